```python
import jax, jax.numpy as jnp
from jax import lax
import numpy as np

D_MODEL = 1024
BATCH = 1
SEQ = 16384
DEPTH = 4
DEC_BATCH = 4
DEC_SEQ = 8192
PAST_LEN = 128

N_MEM = 256
D_CONV_A = 512
CONV_A_W = 3
N_HEADS_B = 8
N_KV_B = 2
HEAD_DIM_B = 64
WINDOW = 128
BLOCK = 128
N_BUCKETS = 32
MAX_DIST = 128
D_Q_B = N_HEADS_B * HEAD_DIM_B
D_KV_B = N_KV_B * HEAD_DIM_B
D_IN_AB = 3 * D_CONV_A + D_Q_B + 2 * D_KV_B
D_MIX_AB = D_CONV_A + D_Q_B
CONV_C_W = 31
N_HEADS_X = 4
HEAD_DIM_X = 128
D_X = N_HEADS_X * HEAD_DIM_X
D_FF = 2816
CONV_F_W = 3
N_EVEN = (DEPTH + 1) // 2
N_ODD = DEPTH // 2
N_NORMS = 6
EPS = 1e-6
NEG = -1e30

kernel_name = "hybrid_conv_swa_conformer_encoder"


def rmsnorm(x, g):
    x32 = x.astype(jnp.float32)
    y = x32 * lax.rsqrt(jnp.mean(x32 * x32, axis=-1, keepdims=True) + EPS)
    return (y * g.astype(jnp.float32)).astype(x.dtype)


def layernorm(x, g, b):
    x32 = x.astype(jnp.float32)
    mu = jnp.mean(x32, axis=-1, keepdims=True)
    xc = x32 - mu
    y = xc * lax.rsqrt(jnp.mean(xc * xc, axis=-1, keepdims=True) + EPS)
    return (y * g.astype(jnp.float32) + b.astype(jnp.float32)).astype(x.dtype)


def dwconv(x, w):
    width, ch = w.shape
    return lax.conv_general_dilated(
        x, w.astype(x.dtype)[:, None, :], window_strides=(1,),
        padding=[(width // 2, width // 2)],
        dimension_numbers=("NWC", "WIO", "NWC"), feature_group_count=ch)


def t5_bucket(rel):
    half = N_BUCKETS // 2
    max_exact = half // 2
    ret = (rel > 0).astype(jnp.int32) * half
    n = jnp.abs(rel)
    nf = jnp.maximum(n, 1).astype(jnp.float32)
    large = max_exact + (jnp.log(nf / max_exact) / np.float32(np.log(MAX_DIST / max_exact))
                         * (half - max_exact)).astype(jnp.int32)
    large = jnp.minimum(large, half - 1)
    return ret + jnp.where(n < max_exact, n, large)


def block_bias(rel_table):
    qi = jnp.arange(BLOCK)[:, None]
    kj = jnp.arange(3 * BLOCK)[None, :]
    rel = kj - BLOCK - qi
    bias = rel_table.astype(jnp.float32)[t5_bucket(rel)]
    return jnp.transpose(bias, (2, 0, 1)), jnp.abs(rel) <= WINDOW


def windowed_gqa(q, k, v, sink, bias, band):
    b, s = q.shape[:2]
    nb = s // BLOCK
    g = N_HEADS_B // N_KV_B
    qb = q.reshape(b, nb, BLOCK, N_KV_B, g, HEAD_DIM_B)

    def band_blocks(t):
        tp = jnp.pad(t, ((0, 0), (BLOCK, BLOCK), (0, 0), (0, 0)))
        return jnp.concatenate(
            [tp[:, o * BLOCK:o * BLOCK + s].reshape(b, nb, BLOCK, N_KV_B, HEAD_DIM_B) for o in range(3)],
            axis=2)

    kb, vb = band_blocks(k), band_blocks(v)
    logits = jnp.einsum("bnqkgd,bnjkd->bnkgqj", qb, kb,
                        preferred_element_type=jnp.float32) * np.float32(HEAD_DIM_B ** -0.5)
    logits = logits + bias.reshape(N_KV_B, g, BLOCK, 3 * BLOCK)
    key_pos = jnp.arange(nb)[:, None] * BLOCK + jnp.arange(3 * BLOCK)[None, :] - BLOCK
    valid = band[None] & ((key_pos >= 0) & (key_pos < s))[:, None, :]
    logits = jnp.where(valid[None, :, None, None], logits, NEG)
    sink_l = sink.astype(jnp.float32).reshape(N_KV_B, g)[None, None, :, :, None, None]
    m = jnp.maximum(jnp.max(logits, axis=-1, keepdims=True), sink_l)
    p = jnp.exp(logits - m)
    denom = jnp.sum(p, axis=-1, keepdims=True) + jnp.exp(sink_l - m)
    o = jnp.einsum("bnkgqj,bnjkd->bnqkgd", (p / denom).astype(v.dtype), vb)
    return o.reshape(b, s, D_Q_B)


def mixer_ab(x, w_in, conv_w, sink, w_out, bias, band):
    b, s, _ = x.shape
    z = x @ w_in
    c0 = D_CONV_A
    gb, gc, xa, q, k, v = jnp.split(
        z, [c0, 2 * c0, 3 * c0, 3 * c0 + D_Q_B, 3 * c0 + D_Q_B + D_KV_B], axis=-1)
    ya = gb * dwconv(gc * xa, conv_w)
    yb = windowed_gqa(q.reshape(b, s, N_HEADS_B, HEAD_DIM_B),
                      k.reshape(b, s, N_KV_B, HEAD_DIM_B),
                      v.reshape(b, s, N_KV_B, HEAD_DIM_B), sink, bias, band)
    return jnp.concatenate([ya, yb], axis=-1) @ w_out


def conformer_conv(x, w_pw1, conv_w, ln_g, ln_b, w_pw2):
    a, gt = jnp.split(x @ w_pw1, 2, axis=-1)
    u = a * jax.nn.sigmoid(gt)
    u = dwconv(u, conv_w)
    u = layernorm(u, ln_g, ln_b)
    return jax.nn.silu(u) @ w_pw2


def mem_xattn(x, mem, w_q, w_kv, w_o):
    b, s, _ = x.shape
    nm = mem.shape[1]
    q = (x @ w_q).reshape(b, s, N_HEADS_X, HEAD_DIM_X)
    k, v = jnp.split(mem @ w_kv, 2, axis=-1)
    k = k.reshape(b, nm, N_HEADS_X, HEAD_DIM_X)
    v = v.reshape(b, nm, N_HEADS_X, HEAD_DIM_X)
    logits = jnp.einsum("bshd,bmhd->bhsm", q, k,
                        preferred_element_type=jnp.float32) * np.float32(HEAD_DIM_X ** -0.5)
    p = jax.nn.softmax(logits, axis=-1).astype(v.dtype)
    o = jnp.einsum("bhsm,bmhd->bshd", p, v).reshape(b, s, D_X)
    return o @ w_o


def conv_ffn(x, w_up, conv_w, w_down):
    h = dwconv(x @ w_up, conv_w)
    g, u = jnp.split(h, 2, axis=-1)
    return (jax.nn.silu(g) * u) @ w_down


def trunk(h, mem, norm_g, rel_bias, w_in_ab, conv_a, sink_b, w_out_ab,
          w_pw1_c, conv_c, ln_g_c, ln_b_c, w_pw2_c, w_xq, w_xkv, w_xo, w_up, conv_f, w_down):
    bias, band = block_bias(rel_bias)
    for i in range(DEPTH):
        g = norm_g[i]
        j = i // 2
        hn = rmsnorm(h, g[0])
        if i % 2 == 0:
            t = mixer_ab(hn, w_in_ab[j], conv_a[j], sink_b[j], w_out_ab[j], bias, band)
        else:
            t = conformer_conv(hn, w_pw1_c[j], conv_c[j], ln_g_c[j], ln_b_c[j], w_pw2_c[j])
        h = h + rmsnorm(t, g[1])
        h = h + rmsnorm(mem_xattn(rmsnorm(h, g[2]), mem, w_xq[i], w_xkv[i], w_xo[i]), g[3])
        h = h + rmsnorm(conv_ffn(rmsnorm(h, g[4]), w_up[i], conv_f[i], w_down[i]), g[5])
    return h


def setup_inputs(seed: int = 0) -> dict:
    key = jax.random.key(seed)
    ks = jax.random.split(key, 24)
    f32 = jnp.float32

    def nrm(k, shape, scale):
        return jax.random.normal(k, shape, f32) * np.float32(scale)

    D = D_MODEL
    return {
        "x_prompt": nrm(ks[0], (BATCH, SEQ, D), 1.0),
        "x_sample": nrm(ks[1], (DEC_BATCH, DEC_SEQ, D), 1.0),
        "mem_prompt": nrm(ks[2], (BATCH, N_MEM, D), 1.0),
        "mem_sample": nrm(ks[3], (DEC_BATCH, N_MEM, D), 1.0),
        "norm_g": 1.0 + nrm(ks[4], (DEPTH, N_NORMS, D), 0.1),
        "rel_bias": nrm(ks[5], (N_BUCKETS, N_HEADS_B), 0.5),
        "w_in_ab": nrm(ks[6], (N_EVEN, D, D_IN_AB), D ** -0.5),
        "conv_a": nrm(ks[7], (N_EVEN, CONV_A_W, D_CONV_A), CONV_A_W ** -0.5),
        "sink_b": nrm(ks[8], (N_EVEN, N_HEADS_B), 0.5),
        "w_out_ab": nrm(ks[9], (N_EVEN, D_MIX_AB, D), D_MIX_AB ** -0.5),
        "w_pw1_c": nrm(ks[10], (N_ODD, D, 2 * D), D ** -0.5),
        "conv_c": nrm(ks[11], (N_ODD, CONV_C_W, D), CONV_C_W ** -0.5),
        "ln_g_c": 1.0 + nrm(ks[12], (N_ODD, D), 0.1),
        "ln_b_c": nrm(ks[13], (N_ODD, D), 0.01),
        "w_pw2_c": nrm(ks[14], (N_ODD, D, D), D ** -0.5),
        "w_xq": nrm(ks[15], (DEPTH, D, D_X), D ** -0.5),
        "w_xkv": nrm(ks[16], (DEPTH, D, 2 * D_X), D ** -0.5),
        "w_xo": nrm(ks[17], (DEPTH, D_X, D), D_X ** -0.5),
        "w_up": nrm(ks[18], (DEPTH, D, 2 * D_FF), D ** -0.5),
        "conv_f": nrm(ks[19], (DEPTH, CONV_F_W, 2 * D_FF), CONV_F_W ** -0.5),
        "w_down": nrm(ks[20], (DEPTH, D_FF, D), D_FF ** -0.5),
    }


def reference(x_prompt, x_sample, mem_prompt, mem_sample, norm_g, rel_bias, w_in_ab, conv_a,
              sink_b, w_out_ab, w_pw1_c, conv_c, ln_g_c, ln_b_c, w_pw2_c, w_xq, w_xkv, w_xo,
              w_up, conv_f, w_down):
    y_prompt = trunk(x_prompt, mem_prompt, norm_g, rel_bias, w_in_ab, conv_a, sink_b, w_out_ab,
                     w_pw1_c, conv_c, ln_g_c, ln_b_c, w_pw2_c, w_xq, w_xkv, w_xo, w_up, conv_f, w_down)
    y_sample = trunk(x_sample, mem_sample, norm_g, rel_bias, w_in_ab, conv_a, sink_b, w_out_ab,
                     w_pw1_c, conv_c, ln_g_c, ln_b_c, w_pw2_c, w_xq, w_xkv, w_xo, w_up, conv_f, w_down)
    return (y_prompt, y_sample)
```

```python
import functools

import numpy as np
import jax
import jax.numpy as jnp
from jax import lax
from jax.experimental import pallas as pl
from jax.experimental.pallas import tpu as pltpu

F32 = jnp.float32
BF16 = jnp.bfloat16

D_MODEL = 1024
DEPTH = 4
D_CONV_A = 512
N_HEADS_B = 8
N_KV_B = 2
HEAD_DIM_B = 64
GROUP_B = N_HEADS_B // N_KV_B
WINDOW = 128
BLOCK = 128
N_BUCKETS = 32
MAX_DIST = 128
D_Q_B = N_HEADS_B * HEAD_DIM_B
CONV_C_W = 31
N_HEADS_X = 4
HEAD_DIM_X = 128
D_X = N_HEADS_X * HEAD_DIM_X
D_FF = 2816
EPS = 1e-6
NEG = -1e30

V7X_VMEM_BYTES = 64 * 1024 * 1024
LANES = 128
SUBLANES_F32 = 8
SUBLANES_BF16 = 16
MXU_DIM = 256

TOKEN_TILE = 512
FF_CHUNK = MXU_DIM
N_FF_CHUNKS = D_FF // FF_CHUNK
CONV_C_HALO = 16
CONV_C_ROWS = 32


def _params(vmem_mib):
    return pltpu.CompilerParams(
        dimension_semantics=("arbitrary", "arbitrary"),
        vmem_limit_bytes=min(vmem_mib * 1024 * 1024, V7X_VMEM_BYTES - 8 * 1024 * 1024))


def _rms(x, g):
    return x * lax.rsqrt(jnp.mean(x * x, axis=-1, keepdims=True) + EPS) * g


def _dot(a, b):
    return jnp.dot(a, b, preferred_element_type=F32)


def _dot_nt(a, b):
    return lax.dot_general(a, b, (((1,), (1,)), ((), ())), preferred_element_type=F32)


def _tile_spec(ts, c):
    return pl.BlockSpec((1, ts, c), lambda b, i: (b, i, 0))


def _prev_spec(ts, hb, c):
    r = ts // hb
    return pl.BlockSpec((1, hb, c), lambda b, i: (b, jnp.maximum(i * r - 1, 0), 0))


def _next_spec(ts, hb, c, s):
    r = ts // hb
    nblk = s // hb
    return pl.BlockSpec((1, hb, c), lambda b, i: (b, jnp.minimum((i + 1) * r, nblk - 1), 0))


def _const_spec(shape):
    nd = len(shape)
    return pl.BlockSpec(shape, lambda b, i: (0,) * nd, pipeline_mode=pl.Buffered(1))


def _inproj_kernel(x_ref, g_ref, w_ref, gb_ref, cx_ref, qlo_ref, qhi_ref, kk_ref, vv_ref):
    hn = _rms(x_ref[0], g_ref[...]).astype(BF16)
    c = D_CONV_A
    gb_ref[0] = _dot(hn, w_ref[:, 0:c]).astype(BF16)
    cx_ref[0] = (_dot(hn, w_ref[:, c:2 * c]) * _dot(hn, w_ref[:, 2 * c:3 * c])).astype(BF16)
    q = _dot(hn, w_ref[:, 3 * c:3 * c + D_Q_B]) * np.float32(HEAD_DIM_B ** -0.5)
    lane = lax.broadcasted_iota(jnp.int32, q.shape, 1)
    even_head = (lane % LANES) < HEAD_DIM_B
    qlo_ref[0] = jnp.where(even_head, q, 0.0).astype(BF16)
    qhi_ref[0] = jnp.where(even_head, 0.0, q).astype(BF16)
    o = 3 * c + D_Q_B
    kk_ref[0] = _dot(hn, w_ref[:, o:o + 2 * LANES]).astype(BF16)
    vv_ref[0] = _dot(hn, w_ref[:, o + 2 * LANES:o + 4 * LANES]).astype(BF16)


def _mixer_kernel(sink_ref, h_ref, gb_ref, cxp_ref, cx_ref, cxn_ref, qlo_ref, qhi_ref,
                  kkp_ref, kk_ref, kkn_ref, vvp_ref, vv_ref, vvn_ref, bias_ref, ca_ref,
                  wout_ref, g_ref, out_ref, ymix_ref):
    i = pl.program_id(1)
    last = pl.num_programs(1) - 1
    tq = h_ref.shape[1]
    nb = tq // BLOCK
    hb = cxp_ref.shape[1]

    cxp = jnp.where(i > 0, cxp_ref[0].astype(F32), 0.0)
    cxn = jnp.where(i < last, cxn_ref[0].astype(F32), 0.0)
    cxe = jnp.concatenate([cxp, cx_ref[0].astype(F32), cxn], axis=0)
    n = tq + 2 * hb
    ca = ca_ref[...]
    conv = (ca[0:1] * pltpu.roll(cxe, 1, 0)[hb:hb + tq]
            + ca[1:2] * cxe[hb:hb + tq]
            + ca[2:3] * pltpu.roll(cxe, n - 1, 0)[hb:hb + tq])
    ymix_ref[:, 0:D_CONV_A] = (gb_ref[0].astype(F32) * conv).astype(BF16)

    kke = jnp.concatenate([kkp_ref[0], kk_ref[0], kkn_ref[0]], axis=0)
    vve = jnp.concatenate([vvp_ref[0], vv_ref[0], vvn_ref[0]], axis=0)
    col = lax.broadcasted_iota(jnp.int32, (1, 3 * BLOCK), 1)
    low_half = lax.broadcasted_iota(jnp.int32, (BLOCK, LANES), 1) < HEAD_DIM_B
    for j in range(nb):
        edge = None
        if j == 0:
            edge = jnp.where((col < BLOCK) & (i == 0), NEG, 0.0).astype(F32)
        if j == nb - 1:
            e2 = jnp.where((col >= 2 * BLOCK) & (i == last), NEG, 0.0).astype(F32)
            edge = e2 if edge is None else edge + e2
        rows = slice(j * BLOCK, (j + 3) * BLOCK)
        qs = slice(j * BLOCK, (j + 1) * BLOCK)
        for kv in range(N_KV_B):
            lanes0 = slice(kv * 2 * LANES, kv * 2 * LANES + LANES)
            lanes1 = slice(kv * 2 * LANES + LANES, (kv + 1) * 2 * LANES)
            kk = kke[rows, kv * LANES:(kv + 1) * LANES]
            vv = vve[rows, kv * LANES:(kv + 1) * LANES]
            q4 = jnp.concatenate([qlo_ref[0, qs, lanes0], qhi_ref[0, qs, lanes0],
                                  qlo_ref[0, qs, lanes1], qhi_ref[0, qs, lanes1]], axis=0)
            s = _dot_nt(q4, kk) + bias_ref[kv]
            if edge is not None:
                s = s + edge
            ps, rs = [], []
            for hh in range(GROUP_B):
                l = s[hh * BLOCK:(hh + 1) * BLOCK]
                sk = sink_ref[kv * GROUP_B + hh]
                m = jnp.maximum(jnp.max(l, axis=-1, keepdims=True), sk)
                p = jnp.exp(l - m)
                den = jnp.sum(p, axis=-1, keepdims=True) + jnp.exp(sk - m)
                ps.append(p.astype(BF16))
                rs.append(1.0 / den)
            o4 = _dot(jnp.concatenate(ps, axis=0), vv) * jnp.concatenate(rs, axis=0)
            c0 = D_CONV_A + kv * 2 * LANES
            ymix_ref[qs, c0:c0 + LANES] = jnp.where(
                low_half, o4[0:BLOCK], o4[BLOCK:2 * BLOCK]).astype(BF16)
            ymix_ref[qs, c0 + LANES:c0 + 2 * LANES] = jnp.where(
                low_half, o4[2 * BLOCK:3 * BLOCK], o4[3 * BLOCK:4 * BLOCK]).astype(BF16)

    t = _dot(ymix_ref[...], wout_ref[...])
    out_ref[0] = h_ref[0] + _rms(t, g_ref[...])


def _mixer_ab(h, g0, g1, w_in, conv_a, sink, w_out, bias4):
    b, s, d = h.shape
    ts = TOKEN_TILE
    grid = (b, s // ts)
    n_in = w_in.shape[1]
    bf = lambda c: jax.ShapeDtypeStruct((b, s, c), BF16)
    gb, cx, qlo, qhi, kk, vv = pl.pallas_call(
        _inproj_kernel,
        grid=grid,
        in_specs=[_tile_spec(ts, d), _const_spec((1, d)), _const_spec((d, n_in))],
        out_specs=[_tile_spec(ts, D_CONV_A), _tile_spec(ts, D_CONV_A), _tile_spec(ts, D_Q_B),
                   _tile_spec(ts, D_Q_B), _tile_spec(ts, 2 * LANES), _tile_spec(ts, 2 * LANES)],
        out_shape=[bf(D_CONV_A), bf(D_CONV_A), bf(D_Q_B), bf(D_Q_B), bf(2 * LANES), bf(2 * LANES)],
        compiler_params=_params(40),
        name="mixer_ab_inproj",
    )(h, g0, w_in)

    hb = SUBLANES_BF16
    return pl.pallas_call(
        _mixer_kernel,
        grid=grid,
        in_specs=[
            pl.BlockSpec(memory_space=pltpu.SMEM),
            _tile_spec(ts, d),
            _tile_spec(ts, D_CONV_A),
            _prev_spec(ts, hb, D_CONV_A), _tile_spec(ts, D_CONV_A), _next_spec(ts, hb, D_CONV_A, s),
            _tile_spec(ts, D_Q_B), _tile_spec(ts, D_Q_B),
            _prev_spec(ts, BLOCK, 2 * LANES), _tile_spec(ts, 2 * LANES), _next_spec(ts, BLOCK, 2 * LANES, s),
            _prev_spec(ts, BLOCK, 2 * LANES), _tile_spec(ts, 2 * LANES), _next_spec(ts, BLOCK, 2 * LANES, s),
            _const_spec(bias4.shape), _const_spec(conv_a.shape), _const_spec(w_out.shape), _const_spec((1, d)),
        ],
        out_specs=_tile_spec(ts, d),
        out_shape=jax.ShapeDtypeStruct((b, s, d), F32),
        scratch_shapes=[pltpu.VMEM((ts, D_CONV_A + D_Q_B), BF16)],
        compiler_params=_params(40),
        name="mixer_ab_core",
    )(sink, h, gb, cx, cx, cx, qlo, qhi, kk, kk, kk, vv, vv, vv, bias4, conv_a, w_out, g1)


def _glu_kernel(x_ref, g_ref, w_ref, u_ref):
    hn = _rms(x_ref[0], g_ref[...]).astype(BF16)
    a = _dot(hn, w_ref[:, 0:D_MODEL])
    gt = _dot(hn, w_ref[:, D_MODEL:2 * D_MODEL])
    u_ref[0] = a * jax.nn.sigmoid(gt)


def _conf_kernel(h_ref, up_ref, u_ref, un_ref, cw_ref, lg_ref, lb_ref, w_ref, g_ref, out_ref,
                 ue_ref, act_ref):
    i = pl.program_id(1)
    last = pl.num_programs(1) - 1
    ts = h_ref.shape[1]
    hb = CONV_C_HALO
    ue_ref[0:hb] = jnp.where(i > 0, up_ref[0], 0.0)
    ue_ref[hb:hb + ts] = u_ref[0]
    ue_ref[hb + ts:hb + ts + hb] = jnp.where(i < last, un_ref[0], 0.0)
    lg = lg_ref[...]
    lb = lb_ref[...]
    rc = CONV_C_ROWS
    off = hb - CONV_C_W // 2
    win = rc + 2 * hb
    sub = SUBLANES_F32

    def rows_step(r, carry):
        r0 = pl.multiple_of(r * rc, rc)
        accs = []
        for lc in range(D_MODEL // LANES):
            lanes = slice(lc * LANES, (lc + 1) * LANES)
            w = ue_ref[pl.ds(r0, win), lanes]
            acc = None
            for res in range(sub):
                wr = w if res == 0 else pltpu.roll(w, win - res, 0)
                for a in range(win // sub):
                    k = sub * a + res - off
                    if 0 <= k < CONV_C_W:
                        term = cw_ref[k:k + 1, lanes] * wr[sub * a:sub * a + rc]
                        acc = term if acc is None else acc + term
            accs.append(acc)
        acc = jnp.concatenate(accs, axis=1)
        mu = jnp.mean(acc, axis=-1, keepdims=True)
        xc = acc - mu
        y = xc * lax.rsqrt(jnp.mean(xc * xc, axis=-1, keepdims=True) + EPS) * lg + lb
        act_ref[pl.ds(r0, rc)] = (y * jax.nn.sigmoid(y)).astype(BF16)
        return carry

    lax.fori_loop(0, ts // rc, rows_step, 0)
    t = _dot(act_ref[...], w_ref[...])
    out_ref[0] = h_ref[0] + _rms(t, g_ref[...])


def _conformer(h, g0, g1, w_pw1, conv_c, ln_g, ln_b, w_pw2):
    b, s, d = h.shape
    ts = TOKEN_TILE
    grid = (b, s // ts)
    u = pl.pallas_call(
        _glu_kernel,
        grid=grid,
        in_specs=[_tile_spec(ts, d), _const_spec((1, d)), _const_spec(w_pw1.shape)],
        out_specs=_tile_spec(ts, d),
        out_shape=jax.ShapeDtypeStruct((b, s, d), F32),
        compiler_params=_params(40),
        name="conformer_glu",
    )(h, g0, w_pw1)
    hb = CONV_C_HALO
    return pl.pallas_call(
        _conf_kernel,
        grid=grid,
        in_specs=[_tile_spec(ts, d), _prev_spec(ts, hb, d), _tile_spec(ts, d), _next_spec(ts, hb, d, s),
                  _const_spec(conv_c.shape), _const_spec((1, d)), _const_spec((1, d)),
                  _const_spec(w_pw2.shape), _const_spec((1, d))],
        out_specs=_tile_spec(ts, d),
        out_shape=jax.ShapeDtypeStruct((b, s, d), F32),
        scratch_shapes=[pltpu.VMEM((ts + 2 * hb, d), F32), pltpu.VMEM((ts, d), BF16)],
        compiler_params=_params(40),
        name="conformer_conv",
    )(h, u, u, u, conv_c, ln_g, ln_b, w_pw2, g1)


def _memkv_kernel(mem_ref, w_ref, k_ref, v_ref):
    kv = _dot(mem_ref[0].astype(BF16), w_ref[0])
    k_ref[0, 0] = kv[:, 0:D_X].astype(BF16)
    v_ref[0, 0] = kv[:, D_X:2 * D_X].astype(BF16)


def _mem_kv(mem, w_xkv):
    b, nm, d = mem.shape
    shape = jax.ShapeDtypeStruct((DEPTH, b, nm, D_X), BF16)
    spec = pl.BlockSpec((1, 1, nm, D_X), lambda l, bb: (l, bb, 0, 0))
    return pl.pallas_call(
        _memkv_kernel,
        grid=(DEPTH, b),
        in_specs=[pl.BlockSpec((1, nm, d), lambda l, bb: (bb, 0, 0)),
                  pl.BlockSpec((1, d, 2 * D_X), lambda l, bb: (l, 0, 0))],
        out_specs=[spec, spec],
        out_shape=[shape, shape],
        compiler_params=_params(32),
        name="mem_kv",
    )(mem, w_xkv)


def _xattn_kernel(h_ref, k_ref, v_ref, wq_ref, wo_ref, g2_ref, g3_ref, out_ref):
    x = h_ref[0]
    hn = _rms(x, g2_ref[...]).astype(BF16)
    q = (_dot(hn, wq_ref[...]) * np.float32(HEAD_DIM_X ** -0.5)).astype(BF16)
    outs = []
    for hh in range(N_HEADS_X):
        lanes = slice(hh * HEAD_DIM_X, (hh + 1) * HEAD_DIM_X)
        s = _dot_nt(q[:, lanes], k_ref[0, 0, :, lanes])
        p = jnp.exp(s - jnp.max(s, axis=-1, keepdims=True))
        r = 1.0 / jnp.sum(p, axis=-1, keepdims=True)
        outs.append((_dot(p.astype(BF16), v_ref[0, 0, :, lanes]) * r).astype(BF16))
    t = _dot(jnp.concatenate(outs, axis=1), wo_ref[...])
    out_ref[0] = x + _rms(t, g3_ref[...])


def _xattn(h, layer, kmem, vmem, g2, g3, w_xq, w_xo):
    b, s, d = h.shape
    ts = TOKEN_TILE
    nm = kmem.shape[2]
    kv_spec = pl.BlockSpec((1, 1, nm, D_X), lambda bb, i: (layer, bb, 0, 0))
    return pl.pallas_call(
        _xattn_kernel,
        grid=(b, s // ts),
        in_specs=[_tile_spec(ts, d), kv_spec, kv_spec, _const_spec(w_xq.shape), _const_spec(w_xo.shape),
                  _const_spec((1, d)), _const_spec((1, d))],
        out_specs=_tile_spec(ts, d),
        out_shape=jax.ShapeDtypeStruct((b, s, d), F32),
        compiler_params=_params(40),
        name="mem_xattn",
    )(h, kmem, vmem, w_xq, w_xo, g2, g3)


def _ffn_kernel(hp_ref, h_ref, hn_ref, g4_ref, wup_ref, cf_ref, wdn_ref, g5_ref, out_ref,
                xn_ref, acc_ref):
    i = pl.program_id(1)
    last = pl.num_programs(1) - 1
    ts = h_ref.shape[1]
    hb = hp_ref.shape[1]
    n = ts + 2 * hb
    g4 = g4_ref[...]
    xn_ref[0:hb] = jnp.where(i > 0, _rms(hp_ref[0], g4), 0.0).astype(BF16)
    xn_ref[hb:hb + ts] = _rms(h_ref[0], g4).astype(BF16)
    xn_ref[hb + ts:n] = jnp.where(i < last, _rms(hn_ref[0], g4), 0.0).astype(BF16)
    acc_ref[...] = jnp.zeros_like(acc_ref)

    def conv3(u, cf):
        return (cf[0:1] * pltpu.roll(u, 1, 0)[hb:hb + ts]
                + cf[1:2] * u[hb:hb + ts]
                + cf[2:3] * pltpu.roll(u, n - 1, 0)[hb:hb + ts])

    def chunk(c, carry):
        xn = xn_ref[...]
        gate = conv3(_dot(xn, wup_ref[c]), cf_ref[c])
        up = conv3(_dot(xn, wup_ref[N_FF_CHUNKS + c]), cf_ref[N_FF_CHUNKS + c])
        act = (gate * jax.nn.sigmoid(gate) * up).astype(BF16)
        acc_ref[...] += _dot(act, wdn_ref[c])
        return carry

    lax.fori_loop(0, N_FF_CHUNKS, chunk, 0)
    out_ref[0] = h_ref[0] + _rms(acc_ref[...], g5_ref[...])


def _conv_ffn(h, g4, g5, w_up, conv_f, w_down):
    b, s, d = h.shape
    ts = TOKEN_TILE
    hb = SUBLANES_BF16
    return pl.pallas_call(
        _ffn_kernel,
        grid=(b, s // ts),
        in_specs=[_prev_spec(ts, hb, d), _tile_spec(ts, d), _next_spec(ts, hb, d, s),
                  _const_spec((1, d)), _const_spec(w_up.shape), _const_spec(conv_f.shape),
                  _const_spec(w_down.shape), _const_spec((1, d))],
        out_specs=_tile_spec(ts, d),
        out_shape=jax.ShapeDtypeStruct((b, s, d), F32),
        scratch_shapes=[pltpu.VMEM((ts + 2 * hb, d), BF16), pltpu.VMEM((ts, d), F32)],
        compiler_params=_params(56),
        name="conv_ffn",
    )(h, h, h, g4, w_up, conv_f, w_down, g5)


def _t5_bucket(rel):
    half = N_BUCKETS // 2
    max_exact = half // 2
    ret = (rel > 0).astype(jnp.int32) * half
    n = jnp.abs(rel)
    nf = jnp.maximum(n, 1).astype(jnp.float32)
    large = max_exact + (jnp.log(nf / max_exact) / np.float32(np.log(MAX_DIST / max_exact))
                         * (half - max_exact)).astype(jnp.int32)
    large = jnp.minimum(large, half - 1)
    return ret + jnp.where(n < max_exact, n, large)


def _band_bias(rel_table):
    qi = jnp.arange(BLOCK)[:, None]
    kj = jnp.arange(3 * BLOCK)[None, :]
    rel = kj - BLOCK - qi
    bias = jnp.transpose(rel_table.astype(F32)[_t5_bucket(rel)], (2, 0, 1))
    bias = jnp.where((jnp.abs(rel) <= WINDOW)[None], bias, NEG)
    return bias.reshape(N_KV_B, GROUP_B * BLOCK, 3 * BLOCK)


def _dup_heads(w):
    d = w.shape[0]
    w = w.reshape(d, N_KV_B, 1, HEAD_DIM_B)
    return jnp.broadcast_to(w, (d, N_KV_B, 2, HEAD_DIM_B)).reshape(d, N_KV_B * 2 * HEAD_DIM_B)


def _prep_w_in(w):
    o = 3 * D_CONV_A + D_Q_B
    dkv = N_KV_B * HEAD_DIM_B
    return jnp.concatenate([w[:, :o], _dup_heads(w[:, o:o + dkv]), _dup_heads(w[:, o + dkv:])],
                           axis=1).astype(BF16)


def _chunk_cols(w):
    r = w.shape[0]
    return jnp.transpose(w.reshape(r, 2 * N_FF_CHUNKS, FF_CHUNK), (1, 0, 2))


def _trunk(h, mem, p):
    kmem, vmem = _mem_kv(mem, p["w_xkv"])
    for i in range(DEPTH):
        g = p["norm_g"][i]
        j = i // 2
        if i % 2 == 0:
            h = _mixer_ab(h, g[0], g[1], p["w_in"][j], p["conv_a"][j], p["sink"][j], p["w_out"][j], p["bias4"])
        else:
            h = _conformer(h, g[0], g[1], p["w_pw1"][j], p["conv_c"][j], p["ln_g"][j], p["ln_b"][j], p["w_pw2"][j])
        h = _xattn(h, i, kmem, vmem, g[2], g[3], p["w_xq"][i], p["w_xo"][i])
        h = _conv_ffn(h, g[4], g[5], p["w_up"][i], p["conv_f"][i], p["w_down"][i])
    return h


def kernel(x_prompt, x_sample, mem_prompt, mem_sample, norm_g, rel_bias, w_in_ab, conv_a, sink_b, w_out_ab,
           w_pw1_c, conv_c, ln_g_c, ln_b_c, w_pw2_c, w_xq, w_xkv, w_xo, w_up, conv_f, w_down):
    n_even = w_in_ab.shape[0]
    n_odd = w_pw1_c.shape[0]
    p = {
        "norm_g": norm_g.astype(F32)[:, :, None, :],
        "bias4": _band_bias(rel_bias),
        "w_in": [_prep_w_in(w_in_ab[j]) for j in range(n_even)],
        "conv_a": conv_a.astype(F32),
        "sink": sink_b.astype(F32),
        "w_out": w_out_ab.astype(BF16),
        "w_pw1": w_pw1_c.astype(BF16),
        "conv_c": conv_c.astype(F32),
        "ln_g": ln_g_c.astype(F32)[:, None, :],
        "ln_b": ln_b_c.astype(F32)[:, None, :],
        "w_pw2": w_pw2_c.astype(BF16),
        "w_xq": w_xq.astype(BF16),
        "w_xkv": w_xkv.astype(BF16),
        "w_xo": w_xo.astype(BF16),
        "w_up": [_chunk_cols(w_up[i]).astype(BF16) for i in range(DEPTH)],
        "conv_f": [_chunk_cols(conv_f[i].astype(F32)) for i in range(DEPTH)],
        "w_down": w_down.astype(BF16).reshape(DEPTH, N_FF_CHUNKS, FF_CHUNK, D_MODEL),
    }
    return (_trunk(x_prompt, mem_prompt, p), _trunk(x_sample, mem_sample, p))
```

```python
import functools

import numpy as np
import jax
import jax.numpy as jnp
from jax import lax
from jax.experimental import pallas as pl
from jax.experimental.pallas import tpu as pltpu

F32 = jnp.float32
BF16 = jnp.bfloat16

D_MODEL = 1024
DEPTH = 4
D_CONV_A = 512
N_HEADS_B = 8
N_KV_B = 2
HEAD_DIM_B = 64
GROUP_B = N_HEADS_B // N_KV_B
WINDOW = 128
BLOCK = 128
N_BUCKETS = 32
MAX_DIST = 128
D_Q_B = N_HEADS_B * HEAD_DIM_B
CONV_C_W = 31
N_HEADS_X = 4
HEAD_DIM_X = 128
D_X = N_HEADS_X * HEAD_DIM_X
D_FF = 2816
EPS = 1e-6
NEG = -1e30

V7X_VMEM_BYTES = 64 * 1024 * 1024
LANES = 128
SUBLANES_F32 = 8
SUBLANES_BF16 = 16
MXU_DIM = 256

TOKEN_TILE = 512
FF_CHUNK = MXU_DIM
N_FF_CHUNKS = D_FF // FF_CHUNK
CONV_C_HALO = 16
CONV_C_ROWS = 32


def _params(vmem_mib):
    return pltpu.CompilerParams(
        dimension_semantics=("arbitrary", "arbitrary"),
        vmem_limit_bytes=min(vmem_mib * 1024 * 1024, V7X_VMEM_BYTES - 8 * 1024 * 1024))


def _rms(x, g):
    return x * lax.rsqrt(jnp.mean(x * x, axis=-1, keepdims=True) + EPS) * g


def _dot(a, b):
    return jnp.dot(a, b, preferred_element_type=F32)


def _dot_nt(a, b):
    return lax.dot_general(a, b, (((1,), (1,)), ((), ())), preferred_element_type=F32)


def _tile_spec(ts, c):
    return pl.BlockSpec((1, ts, c), lambda b, i: (b, i, 0))


def _prev_spec(ts, hb, c):
    r = ts // hb
    return pl.BlockSpec((1, hb, c), lambda b, i: (b, jnp.maximum(i * r - 1, 0), 0))


def _next_spec(ts, hb, c, s):
    r = ts // hb
    nblk = s // hb
    return pl.BlockSpec((1, hb, c), lambda b, i: (b, jnp.minimum((i + 1) * r, nblk - 1), 0))


def _const_spec(shape):
    nd = len(shape)
    return pl.BlockSpec(shape, lambda b, i: (0,) * nd, pipeline_mode=pl.Buffered(1))


def _inproj_kernel(x_ref, g_ref, w_ref, gb_ref, cx_ref, qlo_ref, qhi_ref, kk_ref, vv_ref):
    hn = _rms(x_ref[0], g_ref[...]).astype(BF16)
    c = D_CONV_A
    gb_ref[0] = _dot(hn, w_ref[:, 0:c]).astype(BF16)
    cx_ref[0] = (_dot(hn, w_ref[:, c:2 * c]) * _dot(hn, w_ref[:, 2 * c:3 * c])).astype(BF16)
    q = _dot(hn, w_ref[:, 3 * c:3 * c + D_Q_B]) * np.float32(HEAD_DIM_B ** -0.5)
    lane = lax.broadcasted_iota(jnp.int32, q.shape, 1)
    even_head = (lane % LANES) < HEAD_DIM_B
    qlo_ref[0] = jnp.where(even_head, q, 0.0).astype(BF16)
    qhi_ref[0] = jnp.where(even_head, 0.0, q).astype(BF16)
    o = 3 * c + D_Q_B
    kk_ref[0] = _dot(hn, w_ref[:, o:o + 2 * LANES]).astype(BF16)
    vv_ref[0] = _dot(hn, w_ref[:, o + 2 * LANES:o + 4 * LANES]).astype(BF16)


def _mixer_kernel(sink_ref, h_ref, gb_ref, cxp_ref, cx_ref, cxn_ref, qlo_ref, qhi_ref,
                  kkp_ref, kk_ref, kkn_ref, vvp_ref, vv_ref, vvn_ref, bias_ref, ca_ref,
                  wout_ref, g_ref, out_ref, ymix_ref, cxe_ref):
    i = pl.program_id(1)
    last = pl.num_programs(1) - 1
    tq = h_ref.shape[1]
    nb = tq // BLOCK
    hb = cxp_ref.shape[1]

    for j in range(D_CONV_A // LANES):
        lanes = slice(j * LANES, (j + 1) * LANES)
        cxe_ref[j, 0:hb] = jnp.where(i > 0, cxp_ref[0, :, lanes].astype(F32), 0.0)
        cxe_ref[j, hb:hb + tq] = cx_ref[0, :, lanes].astype(F32)
        cxe_ref[j, hb + tq:hb + tq + hb] = jnp.where(i < last, cxn_ref[0, :, lanes].astype(F32), 0.0)
        conv = (ca_ref[0:1, lanes] * cxe_ref[j, pl.ds(hb - 1, tq), :]
                + ca_ref[1:2, lanes] * cxe_ref[j, pl.ds(hb, tq), :]
                + ca_ref[2:3, lanes] * cxe_ref[j, pl.ds(hb + 1, tq), :])
        ymix_ref[:, lanes] = (gb_ref[0, :, lanes].astype(F32) * conv).astype(BF16)

    kke = jnp.concatenate([kkp_ref[0], kk_ref[0], kkn_ref[0]], axis=0)
    vve = jnp.concatenate([vvp_ref[0], vv_ref[0], vvn_ref[0]], axis=0)
    col = lax.broadcasted_iota(jnp.int32, (1, 3 * BLOCK), 1)
    low_half = lax.broadcasted_iota(jnp.int32, (BLOCK, LANES), 1) < HEAD_DIM_B
    for j in range(nb):
        edge = None
        if j == 0:
            edge = jnp.where((col < BLOCK) & (i == 0), NEG, 0.0).astype(F32)
        if j == nb - 1:
            e2 = jnp.where((col >= 2 * BLOCK) & (i == last), NEG, 0.0).astype(F32)
            edge = e2 if edge is None else edge + e2
        rows = slice(j * BLOCK, (j + 3) * BLOCK)
        qs = slice(j * BLOCK, (j + 1) * BLOCK)
        for kv in range(N_KV_B):
            lanes0 = slice(kv * 2 * LANES, kv * 2 * LANES + LANES)
            lanes1 = slice(kv * 2 * LANES + LANES, (kv + 1) * 2 * LANES)
            kk = kke[rows, kv * LANES:(kv + 1) * LANES]
            vv = vve[rows, kv * LANES:(kv + 1) * LANES]
            q4 = jnp.concatenate([qlo_ref[0, qs, lanes0], qhi_ref[0, qs, lanes0],
                                  qlo_ref[0, qs, lanes1], qhi_ref[0, qs, lanes1]], axis=0)
            s = _dot_nt(q4, kk) + bias_ref[kv]
            if edge is not None:
                s = s + edge
            ps, rs = [], []
            for hh in range(GROUP_B):
                l = s[hh * BLOCK:(hh + 1) * BLOCK]
                sk = sink_ref[kv * GROUP_B + hh]
                m = jnp.maximum(jnp.max(l, axis=-1, keepdims=True), sk)
                p = jnp.exp(l - m)
                den = jnp.sum(p, axis=-1, keepdims=True) + jnp.exp(sk - m)
                ps.append(p.astype(BF16))
                rs.append(1.0 / den)
            o4 = _dot(jnp.concatenate(ps, axis=0), vv) * jnp.concatenate(rs, axis=0)
            c0 = D_CONV_A + kv * 2 * LANES
            ymix_ref[qs, c0:c0 + LANES] = jnp.where(
                low_half, o4[0:BLOCK], o4[BLOCK:2 * BLOCK]).astype(BF16)
            ymix_ref[qs, c0 + LANES:c0 + 2 * LANES] = jnp.where(
                low_half, o4[2 * BLOCK:3 * BLOCK], o4[3 * BLOCK:4 * BLOCK]).astype(BF16)

    t = _dot(ymix_ref[...], wout_ref[...])
    out_ref[0] = h_ref[0] + _rms(t, g_ref[...])


def _mixer_ab(h, g0, g1, w_in, conv_a, sink, w_out, bias4):
    b, s, d = h.shape
    ts = TOKEN_TILE
    grid = (b, s // ts)
    n_in = w_in.shape[1]
    bf = lambda c: jax.ShapeDtypeStruct((b, s, c), BF16)
    gb, cx, qlo, qhi, kk, vv = pl.pallas_call(
        _inproj_kernel,
        grid=grid,
        in_specs=[_tile_spec(ts, d), _const_spec((1, d)), _const_spec((d, n_in))],
        out_specs=[_tile_spec(ts, D_CONV_A), _tile_spec(ts, D_CONV_A), _tile_spec(ts, D_Q_B),
                   _tile_spec(ts, D_Q_B), _tile_spec(ts, 2 * LANES), _tile_spec(ts, 2 * LANES)],
        out_shape=[bf(D_CONV_A), bf(D_CONV_A), bf(D_Q_B), bf(D_Q_B), bf(2 * LANES), bf(2 * LANES)],
        compiler_params=_params(40),
        name="mixer_ab_inproj",
    )(h, g0, w_in)

    hb = SUBLANES_BF16
    return pl.pallas_call(
        _mixer_kernel,
        grid=grid,
        in_specs=[
            pl.BlockSpec(memory_space=pltpu.SMEM),
            _tile_spec(ts, d),
            _tile_spec(ts, D_CONV_A),
            _prev_spec(ts, hb, D_CONV_A), _tile_spec(ts, D_CONV_A), _next_spec(ts, hb, D_CONV_A, s),
            _tile_spec(ts, D_Q_B), _tile_spec(ts, D_Q_B),
            _prev_spec(ts, BLOCK, 2 * LANES), _tile_spec(ts, 2 * LANES), _next_spec(ts, BLOCK, 2 * LANES, s),
            _prev_spec(ts, BLOCK, 2 * LANES), _tile_spec(ts, 2 * LANES), _next_spec(ts, BLOCK, 2 * LANES, s),
            _const_spec(bias4.shape), _const_spec(conv_a.shape), _const_spec(w_out.shape), _const_spec((1, d)),
        ],
        out_specs=_tile_spec(ts, d),
        out_shape=jax.ShapeDtypeStruct((b, s, d), F32),
        scratch_shapes=[pltpu.VMEM((ts, D_CONV_A + D_Q_B), BF16),
                        pltpu.VMEM((D_CONV_A // LANES, ts + 2 * hb, LANES), F32)],
        compiler_params=_params(40),
        name="mixer_ab_core",
    )(sink, h, gb, cx, cx, cx, qlo, qhi, kk, kk, kk, vv, vv, vv, bias4, conv_a, w_out, g1)


def _glu_kernel(x_ref, g_ref, w_ref, u_ref):
    hn = _rms(x_ref[0], g_ref[...]).astype(BF16)
    a = _dot(hn, w_ref[:, 0:D_MODEL])
    gt = _dot(hn, w_ref[:, D_MODEL:2 * D_MODEL])
    u_ref[0] = a * jax.nn.sigmoid(gt)


def _conf_kernel(h_ref, up_ref, u_ref, un_ref, cw_ref, lg_ref, lb_ref, w_ref, g_ref, out_ref,
                 ue_ref, act_ref):
    i = pl.program_id(1)
    last = pl.num_programs(1) - 1
    ts = h_ref.shape[1]
    hb = CONV_C_HALO
    slabs = D_MODEL // LANES
    for j in range(slabs):
        lanes = slice(j * LANES, (j + 1) * LANES)
        ue_ref[j, 0:hb] = jnp.where(i > 0, up_ref[0, :, lanes], 0.0)
        ue_ref[j, hb:hb + ts] = u_ref[0, :, lanes]
        ue_ref[j, hb + ts:hb + ts + hb] = jnp.where(i < last, un_ref[0, :, lanes], 0.0)
    lg = lg_ref[...]
    lb = lb_ref[...]
    rc = CONV_C_ROWS
    off = hb - CONV_C_W // 2

    def rows_step(r, carry):
        r0 = pl.multiple_of(r * rc, rc)
        accs = []
        for j in range(slabs):
            lanes = slice(j * LANES, (j + 1) * LANES)
            acc = cw_ref[0:1, lanes] * ue_ref[j, pl.ds(r0 + off, rc), :]
            for k in range(1, CONV_C_W):
                acc = acc + cw_ref[k:k + 1, lanes] * ue_ref[j, pl.ds(r0 + off + k, rc), :]
            accs.append(acc)
        acc = jnp.concatenate(accs, axis=1)
        mu = jnp.mean(acc, axis=-1, keepdims=True)
        xc = acc - mu
        y = xc * lax.rsqrt(jnp.mean(xc * xc, axis=-1, keepdims=True) + EPS) * lg + lb
        act_ref[pl.ds(r0, rc)] = (y * jax.nn.sigmoid(y)).astype(BF16)
        return carry

    lax.fori_loop(0, ts // rc, rows_step, 0)
    t = _dot(act_ref[...], w_ref[...])
    out_ref[0] = h_ref[0] + _rms(t, g_ref[...])


def _conformer(h, g0, g1, w_pw1, conv_c, ln_g, ln_b, w_pw2):
    b, s, d = h.shape
    ts = TOKEN_TILE
    grid = (b, s // ts)
    u = pl.pallas_call(
        _glu_kernel,
        grid=grid,
        in_specs=[_tile_spec(ts, d), _const_spec((1, d)), _const_spec(w_pw1.shape)],
        out_specs=_tile_spec(ts, d),
        out_shape=jax.ShapeDtypeStruct((b, s, d), F32),
        compiler_params=_params(40),
        name="conformer_glu",
    )(h, g0, w_pw1)
    hb = CONV_C_HALO
    return pl.pallas_call(
        _conf_kernel,
        grid=grid,
        in_specs=[_tile_spec(ts, d), _prev_spec(ts, hb, d), _tile_spec(ts, d), _next_spec(ts, hb, d, s),
                  _const_spec(conv_c.shape), _const_spec((1, d)), _const_spec((1, d)),
                  _const_spec(w_pw2.shape), _const_spec((1, d))],
        out_specs=_tile_spec(ts, d),
        out_shape=jax.ShapeDtypeStruct((b, s, d), F32),
        scratch_shapes=[pltpu.VMEM((d // LANES, ts + 2 * hb, LANES), F32), pltpu.VMEM((ts, d), BF16)],
        compiler_params=_params(40),
        name="conformer_conv",
    )(h, u, u, u, conv_c, ln_g, ln_b, w_pw2, g1)


def _memkv_kernel(mem_ref, w_ref, k_ref, v_ref):
    kv = _dot(mem_ref[0].astype(BF16), w_ref[0])
    k_ref[0, 0] = kv[:, 0:D_X].astype(BF16)
    v_ref[0, 0] = kv[:, D_X:2 * D_X].astype(BF16)


def _mem_kv(mem, w_xkv):
    b, nm, d = mem.shape
    shape = jax.ShapeDtypeStruct((DEPTH, b, nm, D_X), BF16)
    spec = pl.BlockSpec((1, 1, nm, D_X), lambda l, bb: (l, bb, 0, 0))
    return pl.pallas_call(
        _memkv_kernel,
        grid=(DEPTH, b),
        in_specs=[pl.BlockSpec((1, nm, d), lambda l, bb: (bb, 0, 0)),
                  pl.BlockSpec((1, d, 2 * D_X), lambda l, bb: (l, 0, 0))],
        out_specs=[spec, spec],
        out_shape=[shape, shape],
        compiler_params=_params(32),
        name="mem_kv",
    )(mem, w_xkv)


def _xattn_kernel(h_ref, k_ref, v_ref, wq_ref, wo_ref, g2_ref, g3_ref, out_ref):
    x = h_ref[0]
    hn = _rms(x, g2_ref[...]).astype(BF16)
    q = (_dot(hn, wq_ref[...]) * np.float32(HEAD_DIM_X ** -0.5)).astype(BF16)
    outs = []
    for hh in range(N_HEADS_X):
        lanes = slice(hh * HEAD_DIM_X, (hh + 1) * HEAD_DIM_X)
        s = _dot_nt(q[:, lanes], k_ref[0, 0, :, lanes])
        p = jnp.exp(s - jnp.max(s, axis=-1, keepdims=True))
        r = 1.0 / jnp.sum(p, axis=-1, keepdims=True)
        outs.append((_dot(p.astype(BF16), v_ref[0, 0, :, lanes]) * r).astype(BF16))
    t = _dot(jnp.concatenate(outs, axis=1), wo_ref[...])
    out_ref[0] = x + _rms(t, g3_ref[...])


def _xattn(h, layer, kmem, vmem, g2, g3, w_xq, w_xo):
    b, s, d = h.shape
    ts = TOKEN_TILE
    nm = kmem.shape[2]
    kv_spec = pl.BlockSpec((1, 1, nm, D_X), lambda bb, i: (layer, bb, 0, 0))
    return pl.pallas_call(
        _xattn_kernel,
        grid=(b, s // ts),
        in_specs=[_tile_spec(ts, d), kv_spec, kv_spec, _const_spec(w_xq.shape), _const_spec(w_xo.shape),
                  _const_spec((1, d)), _const_spec((1, d))],
        out_specs=_tile_spec(ts, d),
        out_shape=jax.ShapeDtypeStruct((b, s, d), F32),
        compiler_params=_params(40),
        name="mem_xattn",
    )(h, kmem, vmem, w_xq, w_xo, g2, g3)


def _ffn_kernel(hp_ref, h_ref, hn_ref, g4_ref, wup_ref, cf_ref, wdn_ref, g5_ref, out_ref,
                xn_ref, u_ref, act_ref):
    i = pl.program_id(1)
    last = pl.num_programs(1) - 1
    ts = h_ref.shape[1]
    hb = hp_ref.shape[1]
    slabs = 2 * FF_CHUNK // LANES
    x_ext = jnp.concatenate([jnp.where(i > 0, hp_ref[0], 0.0), h_ref[0], jnp.where(i < last, hn_ref[0], 0.0)],
                            axis=0)
    xn_ref[...] = _rms(x_ext, g4_ref[...]).astype(BF16)
    for c in range(N_FF_CHUNKS):
        slot = c % 2
        u = _dot(xn_ref[...], wup_ref[c])
        for j in range(slabs):
            u_ref[slot, j] = u[:, j * LANES:(j + 1) * LANES]

        def conv3(j):
            lanes = slice(j * LANES, (j + 1) * LANES)
            return (cf_ref[c, 0:1, lanes] * u_ref[slot, j, pl.ds(hb - 1, ts), :]
                    + cf_ref[c, 1:2, lanes] * u_ref[slot, j, pl.ds(hb, ts), :]
                    + cf_ref[c, 2:3, lanes] * u_ref[slot, j, pl.ds(hb + 1, ts), :])

        for j in range(slabs // 2):
            gate = conv3(j)
            up = conv3(slabs // 2 + j)
            c0 = c * FF_CHUNK + j * LANES
            act_ref[:, c0:c0 + LANES] = (gate * jax.nn.sigmoid(gate) * up).astype(BF16)
    out_ref[0] = h_ref[0] + _rms(_dot(act_ref[...], wdn_ref[...]), g5_ref[...])


def _conv_ffn(h, g4, g5, w_up, conv_f, w_down):
    b, s, d = h.shape
    ts = TOKEN_TILE
    hb = SUBLANES_F32
    return pl.pallas_call(
        _ffn_kernel,
        grid=(b, s // ts),
        in_specs=[_prev_spec(ts, hb, d), _tile_spec(ts, d), _next_spec(ts, hb, d, s),
                  _const_spec((1, d)), _const_spec(w_up.shape), _const_spec(conv_f.shape),
                  _const_spec(w_down.shape), _const_spec((1, d))],
        out_specs=_tile_spec(ts, d),
        out_shape=jax.ShapeDtypeStruct((b, s, d), F32),
        scratch_shapes=[pltpu.VMEM((ts + 2 * hb, d), BF16),
                        pltpu.VMEM((2, 2 * FF_CHUNK // LANES, ts + 2 * hb, LANES), F32),
                        pltpu.VMEM((ts, D_FF), BF16)],
        compiler_params=_params(56),
        name="conv_ffn",
    )(h, h, h, g4, w_up, conv_f, w_down, g5)


def _t5_bucket(rel):
    half = N_BUCKETS // 2
    max_exact = half // 2
    ret = (rel > 0).astype(jnp.int32) * half
    n = jnp.abs(rel)
    nf = jnp.maximum(n, 1).astype(jnp.float32)
    large = max_exact + (jnp.log(nf / max_exact) / np.float32(np.log(MAX_DIST / max_exact))
                         * (half - max_exact)).astype(jnp.int32)
    large = jnp.minimum(large, half - 1)
    return ret + jnp.where(n < max_exact, n, large)


def _band_bias(rel_table):
    qi = jnp.arange(BLOCK)[:, None]
    kj = jnp.arange(3 * BLOCK)[None, :]
    rel = kj - BLOCK - qi
    hit = _t5_bucket(rel)[None, :, :, None] == jnp.arange(N_BUCKETS)
    bias = jnp.sum(jnp.where(hit, rel_table.astype(F32).T[:, None, None, :], 0.0), axis=-1)
    bias = jnp.where((jnp.abs(rel) <= WINDOW)[None], bias, NEG)
    return bias.reshape(N_KV_B, GROUP_B * BLOCK, 3 * BLOCK)


def _dup_heads(w):
    d = w.shape[0]
    w = w.reshape(d, N_KV_B, 1, HEAD_DIM_B)
    return jnp.broadcast_to(w, (d, N_KV_B, 2, HEAD_DIM_B)).reshape(d, N_KV_B * 2 * HEAD_DIM_B)


def _prep_w_in(w):
    o = 3 * D_CONV_A + D_Q_B
    dkv = N_KV_B * HEAD_DIM_B
    return jnp.concatenate([w[:, :o], _dup_heads(w[:, o:o + dkv]), _dup_heads(w[:, o + dkv:])],
                           axis=1).astype(BF16)


def _chunk_cols(w):
    r = w.shape[0]
    w = w.reshape(r, 2, N_FF_CHUNKS, FF_CHUNK)
    return jnp.transpose(w, (2, 0, 1, 3)).reshape(N_FF_CHUNKS, r, 2 * FF_CHUNK)


def _trunk(h, mem, p):
    kmem, vmem = _mem_kv(mem, p["w_xkv"])
    for i in range(DEPTH):
        g = p["norm_g"][i]
        j = i // 2
        if i % 2 == 0:
            h = _mixer_ab(h, g[0], g[1], p["w_in"][j], p["conv_a"][j], p["sink"][j], p["w_out"][j], p["bias4"])
        else:
            h = _conformer(h, g[0], g[1], p["w_pw1"][j], p["conv_c"][j], p["ln_g"][j], p["ln_b"][j], p["w_pw2"][j])
        h = _xattn(h, i, kmem, vmem, g[2], g[3], p["w_xq"][i], p["w_xo"][i])
        h = _conv_ffn(h, g[4], g[5], p["w_up"][i], p["conv_f"][i], p["w_down"][i])
    return h


def kernel(x_prompt, x_sample, mem_prompt, mem_sample, norm_g, rel_bias, w_in_ab, conv_a, sink_b, w_out_ab,
           w_pw1_c, conv_c, ln_g_c, ln_b_c, w_pw2_c, w_xq, w_xkv, w_xo, w_up, conv_f, w_down):
    n_even = w_in_ab.shape[0]
    n_odd = w_pw1_c.shape[0]
    p = {
        "norm_g": norm_g.astype(F32)[:, :, None, :],
        "bias4": _band_bias(rel_bias),
        "w_in": [_prep_w_in(w_in_ab[j]) for j in range(n_even)],
        "conv_a": conv_a.astype(F32),
        "sink": sink_b.astype(F32),
        "w_out": w_out_ab.astype(BF16),
        "w_pw1": w_pw1_c.astype(BF16),
        "conv_c": conv_c.astype(F32),
        "ln_g": ln_g_c.astype(F32)[:, None, :],
        "ln_b": ln_b_c.astype(F32)[:, None, :],
        "w_pw2": w_pw2_c.astype(BF16),
        "w_xq": w_xq.astype(BF16),
        "w_xkv": w_xkv.astype(BF16),
        "w_xo": w_xo.astype(BF16),
        "w_up": [_chunk_cols(w_up[i]).astype(BF16) for i in range(DEPTH)],
        "conv_f": [_chunk_cols(conv_f[i].astype(F32)) for i in range(DEPTH)],
        "w_down": w_down.astype(BF16),
    }
    return (_trunk(x_prompt, mem_prompt, p), _trunk(x_sample, mem_sample, p))
```

```python
import functools

import numpy as np
import jax
import jax.numpy as jnp
from jax import lax
from jax.experimental import pallas as pl
from jax.experimental.pallas import tpu as pltpu

F32 = jnp.float32
BF16 = jnp.bfloat16

D_MODEL = 1024
DEPTH = 4
D_CONV_A = 512
N_HEADS_B = 8
N_KV_B = 2
HEAD_DIM_B = 64
GROUP_B = N_HEADS_B // N_KV_B
WINDOW = 128
BLOCK = 128
N_BUCKETS = 32
MAX_DIST = 128
D_Q_B = N_HEADS_B * HEAD_DIM_B
CONV_C_W = 31
N_HEADS_X = 4
HEAD_DIM_X = 128
D_X = N_HEADS_X * HEAD_DIM_X
D_FF = 2816
EPS = 1e-6
NEG = -1e30

V7X_VMEM_BYTES = 64 * 1024 * 1024
LANES = 128
SUBLANES_F32 = 8
SUBLANES_BF16 = 16
MXU_DIM = 256

TOKEN_TILE = 1024
FFN_TOKEN_TILE = 512
FF_CHUNK = MXU_DIM
N_FF_CHUNKS = D_FF // FF_CHUNK
CONV_C_HALO = 16
CONV_C_ROWS = 32


def _params(vmem_mib):
    return pltpu.CompilerParams(
        dimension_semantics=("arbitrary", "arbitrary"),
        vmem_limit_bytes=min(vmem_mib * 1024 * 1024, V7X_VMEM_BYTES - 8 * 1024 * 1024))


def _rms(x, g):
    return x * lax.rsqrt(jnp.mean(x * x, axis=-1, keepdims=True) + EPS) * g


def _dot(a, b):
    return jnp.dot(a, b, preferred_element_type=F32)


def _dot_nt(a, b):
    return lax.dot_general(a, b, (((1,), (1,)), ((), ())), preferred_element_type=F32)


def _tile_spec(ts, c):
    return pl.BlockSpec((1, ts, c), lambda b, i: (b, i, 0))


def _prev_spec(ts, hb, c):
    r = ts // hb
    return pl.BlockSpec((1, hb, c), lambda b, i: (b, jnp.maximum(i * r - 1, 0), 0))


def _next_spec(ts, hb, c, s):
    r = ts // hb
    nblk = s // hb
    return pl.BlockSpec((1, hb, c), lambda b, i: (b, jnp.minimum((i + 1) * r, nblk - 1), 0))


def _const_spec(shape):
    nd = len(shape)
    return pl.BlockSpec(shape, lambda b, i: (0,) * nd, pipeline_mode=pl.Buffered(1))


def _layer_spec(stacked_shape, layer):
    nd = len(stacked_shape) - 1
    return pl.BlockSpec((None,) + tuple(stacked_shape[1:]), lambda b, i: (layer,) + (0,) * nd,
                        pipeline_mode=pl.Buffered(1))


def _inproj_kernel(x_ref, g_ref, w_ref, gb_ref, cx_ref, qlo_ref, qhi_ref, kk_ref, vv_ref):
    hn = _rms(x_ref[0], g_ref[...]).astype(BF16)
    c = D_CONV_A
    gb_ref[0] = _dot(hn, w_ref[:, 0:c]).astype(BF16)
    cx_ref[0] = (_dot(hn, w_ref[:, c:2 * c]) * _dot(hn, w_ref[:, 2 * c:3 * c])).astype(BF16)
    q = _dot(hn, w_ref[:, 3 * c:3 * c + D_Q_B]) * np.float32(HEAD_DIM_B ** -0.5)
    lane = lax.broadcasted_iota(jnp.int32, q.shape, 1)
    even_head = (lane % LANES) < HEAD_DIM_B
    qlo_ref[0] = jnp.where(even_head, q, 0.0).astype(BF16)
    qhi_ref[0] = jnp.where(even_head, 0.0, q).astype(BF16)
    o = 3 * c + D_Q_B
    kk_ref[0] = _dot(hn, w_ref[:, o:o + 2 * LANES]).astype(BF16)
    vv_ref[0] = _dot(hn, w_ref[:, o + 2 * LANES:o + 4 * LANES]).astype(BF16)


def _mixer_kernel(sink_ref, h_ref, gb_ref, cxp_ref, cx_ref, cxn_ref, qlo_ref, qhi_ref,
                  kkp_ref, kk_ref, kkn_ref, vvp_ref, vv_ref, vvn_ref, bias_ref, ca_ref,
                  wout_ref, g_ref, out_ref, ymix_ref, cxe_ref):
    i = pl.program_id(1)
    last = pl.num_programs(1) - 1
    tq = h_ref.shape[1]
    nb = tq // BLOCK
    hb = cxp_ref.shape[1]

    for j in range(D_CONV_A // LANES):
        lanes = slice(j * LANES, (j + 1) * LANES)
        cxe_ref[j, 0:hb] = jnp.where(i > 0, cxp_ref[0, :, lanes].astype(F32), 0.0)
        cxe_ref[j, hb:hb + tq] = cx_ref[0, :, lanes].astype(F32)
        cxe_ref[j, hb + tq:hb + tq + hb] = jnp.where(i < last, cxn_ref[0, :, lanes].astype(F32), 0.0)
        conv = (ca_ref[0:1, lanes] * cxe_ref[j, pl.ds(hb - 1, tq), :]
                + ca_ref[1:2, lanes] * cxe_ref[j, pl.ds(hb, tq), :]
                + ca_ref[2:3, lanes] * cxe_ref[j, pl.ds(hb + 1, tq), :])
        ymix_ref[:, lanes] = (gb_ref[0, :, lanes].astype(F32) * conv).astype(BF16)

    kke = jnp.concatenate([kkp_ref[0], kk_ref[0], kkn_ref[0]], axis=0)
    vve = jnp.concatenate([vvp_ref[0], vv_ref[0], vvn_ref[0]], axis=0)
    col = lax.broadcasted_iota(jnp.int32, (1, 3 * BLOCK), 1)
    low_half = lax.broadcasted_iota(jnp.int32, (BLOCK, LANES), 1) < HEAD_DIM_B
    for j in range(nb):
        edge = None
        if j == 0:
            edge = jnp.where((col < BLOCK) & (i == 0), NEG, 0.0).astype(F32)
        if j == nb - 1:
            e2 = jnp.where((col >= 2 * BLOCK) & (i == last), NEG, 0.0).astype(F32)
            edge = e2 if edge is None else edge + e2
        rows = slice(j * BLOCK, (j + 3) * BLOCK)
        qs = slice(j * BLOCK, (j + 1) * BLOCK)
        for kv in range(N_KV_B):
            lanes0 = slice(kv * 2 * LANES, kv * 2 * LANES + LANES)
            lanes1 = slice(kv * 2 * LANES + LANES, (kv + 1) * 2 * LANES)
            kk = kke[rows, kv * LANES:(kv + 1) * LANES]
            vv = vve[rows, kv * LANES:(kv + 1) * LANES]
            q4 = jnp.concatenate([qlo_ref[0, qs, lanes0], qhi_ref[0, qs, lanes0],
                                  qlo_ref[0, qs, lanes1], qhi_ref[0, qs, lanes1]], axis=0)
            s = _dot_nt(q4, kk) + bias_ref[kv]
            if edge is not None:
                s = s + edge
            ps, rs = [], []
            for hh in range(GROUP_B):
                l = s[hh * BLOCK:(hh + 1) * BLOCK]
                sk = sink_ref[kv * GROUP_B + hh]
                m = jnp.maximum(jnp.max(l, axis=-1, keepdims=True), sk)
                p = jnp.exp(l - m)
                den = jnp.sum(p, axis=-1, keepdims=True) + jnp.exp(sk - m)
                ps.append(p.astype(BF16))
                rs.append(1.0 / den)
            o4 = _dot(jnp.concatenate(ps, axis=0), vv) * jnp.concatenate(rs, axis=0)
            c0 = D_CONV_A + kv * 2 * LANES
            ymix_ref[qs, c0:c0 + LANES] = jnp.where(
                low_half, o4[0:BLOCK], o4[BLOCK:2 * BLOCK]).astype(BF16)
            ymix_ref[qs, c0 + LANES:c0 + 2 * LANES] = jnp.where(
                low_half, o4[2 * BLOCK:3 * BLOCK], o4[3 * BLOCK:4 * BLOCK]).astype(BF16)

    t = _dot(ymix_ref[...], wout_ref[...])
    out_ref[0] = h_ref[0] + _rms(t, g_ref[...])


def _mixer_ab(h, g0, g1, w_in, conv_a, sink, w_out, bias4):
    b, s, d = h.shape
    ts = TOKEN_TILE
    grid = (b, s // ts)
    n_in = w_in.shape[1]
    bf = lambda c: jax.ShapeDtypeStruct((b, s, c), BF16)
    gb, cx, qlo, qhi, kk, vv = pl.pallas_call(
        _inproj_kernel,
        grid=grid,
        in_specs=[_tile_spec(ts, d), _const_spec((1, d)), _const_spec((d, n_in))],
        out_specs=[_tile_spec(ts, D_CONV_A), _tile_spec(ts, D_CONV_A), _tile_spec(ts, D_Q_B),
                   _tile_spec(ts, D_Q_B), _tile_spec(ts, 2 * LANES), _tile_spec(ts, 2 * LANES)],
        out_shape=[bf(D_CONV_A), bf(D_CONV_A), bf(D_Q_B), bf(D_Q_B), bf(2 * LANES), bf(2 * LANES)],
        compiler_params=_params(48),
        name="mixer_ab_inproj",
    )(h, g0, w_in)

    hb = SUBLANES_BF16
    return pl.pallas_call(
        _mixer_kernel,
        grid=grid,
        in_specs=[
            pl.BlockSpec(memory_space=pltpu.SMEM),
            _tile_spec(ts, d),
            _tile_spec(ts, D_CONV_A),
            _prev_spec(ts, hb, D_CONV_A), _tile_spec(ts, D_CONV_A), _next_spec(ts, hb, D_CONV_A, s),
            _tile_spec(ts, D_Q_B), _tile_spec(ts, D_Q_B),
            _prev_spec(ts, BLOCK, 2 * LANES), _tile_spec(ts, 2 * LANES), _next_spec(ts, BLOCK, 2 * LANES, s),
            _prev_spec(ts, BLOCK, 2 * LANES), _tile_spec(ts, 2 * LANES), _next_spec(ts, BLOCK, 2 * LANES, s),
            _const_spec(bias4.shape), _const_spec(conv_a.shape), _const_spec(w_out.shape), _const_spec((1, d)),
        ],
        out_specs=_tile_spec(ts, d),
        out_shape=jax.ShapeDtypeStruct((b, s, d), F32),
        scratch_shapes=[pltpu.VMEM((ts, D_CONV_A + D_Q_B), BF16),
                        pltpu.VMEM((D_CONV_A // LANES, ts + 2 * hb, LANES), F32)],
        compiler_params=_params(48),
        name="mixer_ab_core",
    )(sink, h, gb, cx, cx, cx, qlo, qhi, kk, kk, kk, vv, vv, vv, bias4, conv_a, w_out, g1)


def _slab_spec(rows, row_block):
    return pl.BlockSpec((1, D_MODEL // LANES, rows, LANES), lambda b, i: (b, 0, row_block(i), 0))


def _glu_kernel(x_ref, g_ref, w_ref, u_ref):
    hn = _rms(x_ref[0], g_ref[...]).astype(BF16)
    u = _dot(hn, w_ref[:, 0:D_MODEL]) * jax.nn.sigmoid(_dot(hn, w_ref[:, D_MODEL:2 * D_MODEL]))
    for j in range(D_MODEL // LANES):
        u_ref[0, j] = u[:, j * LANES:(j + 1) * LANES]


def _conf_kernel(h_ref, up_ref, u_ref, un_ref, cw_ref, lg_ref, lb_ref, w_ref, g_ref, out_ref,
                 edge_ref, cv_ref):
    i = pl.program_id(1)
    last = pl.num_programs(1) - 1
    ts = h_ref.shape[1]
    hb = CONV_C_HALO
    rc = CONV_C_ROWS
    pad = CONV_C_W // 2
    slabs = D_MODEL // LANES
    for j in range(slabs):
        edge_ref[0, j, 0:hb] = jnp.where(i > 0, up_ref[0, j], 0.0)
        edge_ref[0, j, hb:2 * hb + rc] = u_ref[0, j, 0:hb + rc]
        edge_ref[1, j, 0:hb + rc] = u_ref[0, j, ts - hb - rc:ts]
        edge_ref[1, j, hb + rc:2 * hb + rc] = jnp.where(i < last, un_ref[0, j], 0.0)

    def conv_rows(load, r0):
        for j in range(slabs):
            lanes = slice(j * LANES, (j + 1) * LANES)
            acc = [cw_ref[k:k + 1, lanes] * load(j, k) for k in range(2)]
            for k in range(2, CONV_C_W):
                acc[k % 2] = acc[k % 2] + cw_ref[k:k + 1, lanes] * load(j, k)
            cv_ref[pl.ds(r0, rc), lanes] = acc[0] + acc[1]

    conv_rows(lambda j, k: edge_ref[0, j, pl.ds(k + hb - pad, rc), :], 0)
    conv_rows(lambda j, k: edge_ref[1, j, pl.ds(k + hb - pad, rc), :], ts - rc)

    def rows_step(r, carry):
        r0 = pl.multiple_of(r * rc, rc)
        conv_rows(lambda j, k: u_ref[0, j, pl.ds(r0 + (k - pad), rc), :], r0)
        return carry

    lax.fori_loop(1, ts // rc - 1, rows_step, 0)
    x = cv_ref[...]
    xc = x - jnp.mean(x, axis=-1, keepdims=True)
    y = xc * lax.rsqrt(jnp.mean(xc * xc, axis=-1, keepdims=True) + EPS) * lg_ref[...] + lb_ref[...]
    t = _dot((y * jax.nn.sigmoid(y)).astype(BF16), w_ref[...])
    out_ref[0] = h_ref[0] + _rms(t, g_ref[...])


def _conformer(h, g0, g1, w_pw1, conv_c, ln_g, ln_b, w_pw2):
    b, s, d = h.shape
    ts = TOKEN_TILE
    grid = (b, s // ts)
    u = pl.pallas_call(
        _glu_kernel,
        grid=grid,
        in_specs=[_tile_spec(ts, d), _const_spec((1, d)), _const_spec(w_pw1.shape)],
        out_specs=_slab_spec(ts, lambda i: i),
        out_shape=jax.ShapeDtypeStruct((b, d // LANES, s, LANES), F32),
        compiler_params=_params(48),
        name="conformer_glu",
    )(h, g0, w_pw1)
    hb = CONV_C_HALO
    r = ts // hb
    nblk = s // hb
    return pl.pallas_call(
        _conf_kernel,
        grid=grid,
        in_specs=[_tile_spec(ts, d),
                  _slab_spec(hb, lambda i: jnp.maximum(i * r - 1, 0)),
                  _slab_spec(ts, lambda i: i),
                  _slab_spec(hb, lambda i: jnp.minimum((i + 1) * r, nblk - 1)),
                  _const_spec(conv_c.shape), _const_spec((1, d)), _const_spec((1, d)),
                  _const_spec(w_pw2.shape), _const_spec((1, d))],
        out_specs=_tile_spec(ts, d),
        out_shape=jax.ShapeDtypeStruct((b, s, d), F32),
        scratch_shapes=[pltpu.VMEM((2, d // LANES, CONV_C_ROWS + 2 * hb, LANES), F32), pltpu.VMEM((ts, d), F32)],
        compiler_params=_params(48),
        name="conformer_conv",
    )(h, u, u, u, conv_c, ln_g, ln_b, w_pw2, g1)


def _memkv_kernel(mem_ref, w_ref, k_ref, v_ref):
    kv = _dot(mem_ref[0].astype(BF16), w_ref[0])
    k_ref[0, 0] = kv[:, 0:D_X].astype(BF16)
    v_ref[0, 0] = kv[:, D_X:2 * D_X].astype(BF16)


def _mem_kv(mem, w_xkv):
    b, nm, d = mem.shape
    shape = jax.ShapeDtypeStruct((DEPTH, b, nm, D_X), BF16)
    spec = pl.BlockSpec((1, 1, nm, D_X), lambda l, bb: (l, bb, 0, 0))
    return pl.pallas_call(
        _memkv_kernel,
        grid=(DEPTH, b),
        in_specs=[pl.BlockSpec((1, nm, d), lambda l, bb: (bb, 0, 0)),
                  pl.BlockSpec((1, d, 2 * D_X), lambda l, bb: (l, 0, 0))],
        out_specs=[spec, spec],
        out_shape=[shape, shape],
        compiler_params=_params(32),
        name="mem_kv",
    )(mem, w_xkv)


def _xattn_kernel(h_ref, k_ref, v_ref, wq_ref, wo_ref, g2_ref, g3_ref, out_ref):
    x = h_ref[0]
    hn = _rms(x, g2_ref[...]).astype(BF16)
    q = (_dot(hn, wq_ref[...]) * np.float32(HEAD_DIM_X ** -0.5)).astype(BF16)
    outs = []
    for hh in range(N_HEADS_X):
        lanes = slice(hh * HEAD_DIM_X, (hh + 1) * HEAD_DIM_X)
        s = _dot_nt(q[:, lanes], k_ref[0, 0, :, lanes])
        p = jnp.exp(s - jnp.max(s, axis=-1, keepdims=True))
        r = 1.0 / jnp.sum(p, axis=-1, keepdims=True)
        outs.append((_dot(p.astype(BF16), v_ref[0, 0, :, lanes]) * r).astype(BF16))
    t = _dot(jnp.concatenate(outs, axis=1), wo_ref[...])
    out_ref[0] = x + _rms(t, g3_ref[...])


def _xattn(h, layer, kmem, vmem, g2, g3, w_xq, w_xo):
    b, s, d = h.shape
    ts = TOKEN_TILE
    nm = kmem.shape[2]
    kv_spec = pl.BlockSpec((1, 1, nm, D_X), lambda bb, i: (layer, bb, 0, 0))
    return pl.pallas_call(
        _xattn_kernel,
        grid=(b, s // ts),
        in_specs=[_tile_spec(ts, d), kv_spec, kv_spec, _layer_spec(w_xq.shape, layer),
                  _layer_spec(w_xo.shape, layer), _const_spec((1, d)), _const_spec((1, d))],
        out_specs=_tile_spec(ts, d),
        out_shape=jax.ShapeDtypeStruct((b, s, d), F32),
        compiler_params=_params(48),
        name="mem_xattn",
    )(h, kmem, vmem, w_xq, w_xo, g2, g3)


def _ffn_kernel(hp_ref, h_ref, hn_ref, g4_ref, wup_ref, cf_ref, wdn_ref, g5_ref, out_ref,
                xn_ref, u_ref, act_ref):
    i = pl.program_id(1)
    last = pl.num_programs(1) - 1
    ts = h_ref.shape[1]
    hb = hp_ref.shape[1]
    half = FF_CHUNK // LANES
    x_ext = jnp.concatenate([jnp.where(i > 0, hp_ref[0], 0.0), h_ref[0], jnp.where(i < last, hn_ref[0], 0.0)],
                            axis=0)
    xn_ref[...] = _rms(x_ext, g4_ref[...]).astype(BF16)
    for c in range(N_FF_CHUNKS):
        slot = c % 2
        cols = (c * FF_CHUNK, D_FF + c * FF_CHUNK)
        for part, c0 in enumerate(cols):
            u = _dot(xn_ref[...], wup_ref[:, c0:c0 + FF_CHUNK])
            for j in range(half):
                u_ref[slot, part * half + j] = u[:, j * LANES:(j + 1) * LANES]

        def conv3(part, j):
            lanes = slice(cols[part] + j * LANES, cols[part] + (j + 1) * LANES)
            slab = part * half + j
            return (cf_ref[0:1, lanes] * u_ref[slot, slab, pl.ds(hb - 1, ts), :]
                    + cf_ref[1:2, lanes] * u_ref[slot, slab, pl.ds(hb, ts), :]
                    + cf_ref[2:3, lanes] * u_ref[slot, slab, pl.ds(hb + 1, ts), :])

        for j in range(half):
            gate = conv3(0, j)
            up = conv3(1, j)
            c0 = c * FF_CHUNK + j * LANES
            act_ref[:, c0:c0 + LANES] = (gate * jax.nn.sigmoid(gate) * up).astype(BF16)
    out_ref[0] = h_ref[0] + _rms(_dot(act_ref[...], wdn_ref[...]), g5_ref[...])


def _conv_ffn(h, layer, g4, g5, w_up, conv_f, w_down):
    b, s, d = h.shape
    ts = FFN_TOKEN_TILE
    hb = SUBLANES_F32
    return pl.pallas_call(
        _ffn_kernel,
        grid=(b, s // ts),
        in_specs=[_prev_spec(ts, hb, d), _tile_spec(ts, d), _next_spec(ts, hb, d, s),
                  _const_spec((1, d)), _layer_spec(w_up.shape, layer), _layer_spec(conv_f.shape, layer),
                  _layer_spec(w_down.shape, layer), _const_spec((1, d))],
        out_specs=_tile_spec(ts, d),
        out_shape=jax.ShapeDtypeStruct((b, s, d), F32),
        scratch_shapes=[pltpu.VMEM((ts + 2 * hb, d), BF16),
                        pltpu.VMEM((2, 2 * FF_CHUNK // LANES, ts + 2 * hb, LANES), F32),
                        pltpu.VMEM((ts, D_FF), BF16)],
        compiler_params=_params(56),
        name="conv_ffn",
    )(h, h, h, g4, w_up, conv_f, w_down, g5)


def _t5_bucket(rel):
    half = N_BUCKETS // 2
    max_exact = half // 2
    ret = (rel > 0).astype(jnp.int32) * half
    n = jnp.abs(rel)
    nf = jnp.maximum(n, 1).astype(jnp.float32)
    large = max_exact + (jnp.log(nf / max_exact) / np.float32(np.log(MAX_DIST / max_exact))
                         * (half - max_exact)).astype(jnp.int32)
    large = jnp.minimum(large, half - 1)
    return ret + jnp.where(n < max_exact, n, large)


def _band_bias(rel_table):
    qi = jnp.arange(BLOCK)[:, None]
    kj = jnp.arange(3 * BLOCK)[None, :]
    rel = kj - BLOCK - qi
    hit = _t5_bucket(rel)[None, :, :, None] == jnp.arange(N_BUCKETS)
    bias = jnp.sum(jnp.where(hit, rel_table.astype(F32).T[:, None, None, :], 0.0), axis=-1)
    bias = jnp.where((jnp.abs(rel) <= WINDOW)[None], bias, NEG)
    return bias.reshape(N_KV_B, GROUP_B * BLOCK, 3 * BLOCK)


def _dup_heads(w):
    d = w.shape[0]
    w = w.reshape(d, N_KV_B, 1, HEAD_DIM_B)
    return jnp.broadcast_to(w, (d, N_KV_B, 2, HEAD_DIM_B)).reshape(d, N_KV_B * 2 * HEAD_DIM_B)


def _prep_w_in(w):
    o = 3 * D_CONV_A + D_Q_B
    dkv = N_KV_B * HEAD_DIM_B
    return jnp.concatenate([w[:, :o], _dup_heads(w[:, o:o + dkv]), _dup_heads(w[:, o + dkv:])],
                           axis=1).astype(BF16)


def _trunk(h, mem, p):
    kmem, vmem = _mem_kv(mem, p["w_xkv"])
    for i in range(DEPTH):
        g = p["norm_g"][i]
        j = i // 2
        if i % 2 == 0:
            h = _mixer_ab(h, g[0], g[1], p["w_in"][j], p["conv_a"][j], p["sink"][j], p["w_out"][j], p["bias4"])
        else:
            h = _conformer(h, g[0], g[1], p["w_pw1"][j], p["conv_c"][j], p["ln_g"][j], p["ln_b"][j], p["w_pw2"][j])
        h = _xattn(h, i, kmem, vmem, g[2], g[3], p["w_xq"], p["w_xo"])
        h = _conv_ffn(h, i, g[4], g[5], p["w_up"], p["conv_f"], p["w_down"])
    return h


def kernel(x_prompt, x_sample, mem_prompt, mem_sample, norm_g, rel_bias, w_in_ab, conv_a, sink_b, w_out_ab,
           w_pw1_c, conv_c, ln_g_c, ln_b_c, w_pw2_c, w_xq, w_xkv, w_xo, w_up, conv_f, w_down):
    n_even = w_in_ab.shape[0]
    n_odd = w_pw1_c.shape[0]
    p = {
        "norm_g": norm_g.astype(F32)[:, :, None, :],
        "bias4": _band_bias(rel_bias),
        "w_in": [_prep_w_in(w_in_ab[j]) for j in range(n_even)],
        "conv_a": conv_a.astype(F32),
        "sink": sink_b.astype(F32),
        "w_out": w_out_ab.astype(BF16),
        "w_pw1": w_pw1_c.astype(BF16),
        "conv_c": conv_c.astype(F32),
        "ln_g": ln_g_c.astype(F32)[:, None, :],
        "ln_b": ln_b_c.astype(F32)[:, None, :],
        "w_pw2": w_pw2_c.astype(BF16),
        "w_xq": w_xq.astype(BF16),
        "w_xkv": w_xkv.astype(BF16),
        "w_xo": w_xo.astype(BF16),
        "w_up": w_up.astype(BF16),
        "conv_f": conv_f.astype(F32),
        "w_down": w_down.astype(BF16),
    }
    return (_trunk(x_prompt, mem_prompt, p), _trunk(x_sample, mem_sample, p))
```

```python
import functools

import numpy as np
import jax
import jax.numpy as jnp
from jax import lax
from jax.experimental import pallas as pl
from jax.experimental.pallas import tpu as pltpu

F32 = jnp.float32
BF16 = jnp.bfloat16

D_MODEL = 1024
DEPTH = 4
D_CONV_A = 512
N_HEADS_B = 8
N_KV_B = 2
HEAD_DIM_B = 64
GROUP_B = N_HEADS_B // N_KV_B
WINDOW = 128
BLOCK = 128
N_BUCKETS = 32
MAX_DIST = 128
D_Q_B = N_HEADS_B * HEAD_DIM_B
CONV_C_W = 31
N_HEADS_X = 4
HEAD_DIM_X = 128
D_X = N_HEADS_X * HEAD_DIM_X
D_FF = 2816
EPS = 1e-6
NEG = -1e30

V7X_VMEM_BYTES = 64 * 1024 * 1024
LANES = 128
SUBLANES_F32 = 8
SUBLANES_BF16 = 16
MXU_DIM = 256

TOKEN_TILE = 1024
FFN_TOKEN_TILE = 512
FF_CHUNK = MXU_DIM
N_FF_CHUNKS = D_FF // FF_CHUNK
CONV_C_HALO = 16
CONV_C_ROWS = 32


def _params(vmem_mib):
    return pltpu.CompilerParams(
        dimension_semantics=("arbitrary", "arbitrary"),
        vmem_limit_bytes=min(vmem_mib * 1024 * 1024, V7X_VMEM_BYTES - 8 * 1024 * 1024))


def _rms(x, g):
    return x * lax.rsqrt(jnp.mean(x * x, axis=-1, keepdims=True) + EPS) * g


def _dot(a, b):
    return jnp.dot(a, b, preferred_element_type=F32)


def _dot_nt(a, b):
    return lax.dot_general(a, b, (((1,), (1,)), ((), ())), preferred_element_type=F32)


def _tile_spec(ts, c):
    return pl.BlockSpec((1, ts, c), lambda b, i: (b, i, 0))


def _prev_spec(ts, hb, c):
    r = ts // hb
    return pl.BlockSpec((1, hb, c), lambda b, i: (b, jnp.maximum(i * r - 1, 0), 0))


def _next_spec(ts, hb, c, s):
    r = ts // hb
    nblk = s // hb
    return pl.BlockSpec((1, hb, c), lambda b, i: (b, jnp.minimum((i + 1) * r, nblk - 1), 0))


def _const_spec(shape):
    nd = len(shape)
    return pl.BlockSpec(shape, lambda b, i: (0,) * nd, pipeline_mode=pl.Buffered(1))


def _layer_spec(stacked_shape, layer):
    nd = len(stacked_shape) - 1
    return pl.BlockSpec((None,) + tuple(stacked_shape[1:]), lambda b, i: (layer,) + (0,) * nd,
                        pipeline_mode=pl.Buffered(1))


def _inproj_kernel(x_ref, g_ref, w_ref, gb_ref, cx_ref, qlo_ref, qhi_ref, kk_ref, vv_ref):
    hn = _rms(x_ref[0], g_ref[...]).astype(BF16)
    c = D_CONV_A
    gb_ref[0] = _dot(hn, w_ref[:, 0:c]).astype(BF16)
    cx_ref[0] = (_dot(hn, w_ref[:, c:2 * c]) * _dot(hn, w_ref[:, 2 * c:3 * c])).astype(BF16)
    q = _dot(hn, w_ref[:, 3 * c:3 * c + D_Q_B]) * np.float32(HEAD_DIM_B ** -0.5)
    lane = lax.broadcasted_iota(jnp.int32, q.shape, 1)
    even_head = (lane % LANES) < HEAD_DIM_B
    qlo_ref[0] = jnp.where(even_head, q, 0.0).astype(BF16)
    qhi_ref[0] = jnp.where(even_head, 0.0, q).astype(BF16)
    o = 3 * c + D_Q_B
    kk_ref[0] = _dot(hn, w_ref[:, o:o + 2 * LANES]).astype(BF16)
    vv_ref[0] = _dot(hn, w_ref[:, o + 2 * LANES:o + 4 * LANES]).astype(BF16)


def _mixer_kernel(sink_ref, h_ref, gb_ref, cxp_ref, cx_ref, cxn_ref, qlo_ref, qhi_ref,
                  kkp_ref, kk_ref, kkn_ref, vvp_ref, vv_ref, vvn_ref, bias_ref, ca_ref,
                  wout_ref, g_ref, out_ref, ymix_ref, cxe_ref):
    i = pl.program_id(1)
    last = pl.num_programs(1) - 1
    tq = h_ref.shape[1]
    nb = tq // BLOCK
    hb = cxp_ref.shape[1]

    for j in range(D_CONV_A // LANES):
        lanes = slice(j * LANES, (j + 1) * LANES)
        cxe_ref[j, 0:hb] = jnp.where(i > 0, cxp_ref[0, :, lanes].astype(F32), 0.0)
        cxe_ref[j, hb:hb + tq] = cx_ref[0, :, lanes].astype(F32)
        cxe_ref[j, hb + tq:hb + tq + hb] = jnp.where(i < last, cxn_ref[0, :, lanes].astype(F32), 0.0)
        conv = (ca_ref[0:1, lanes] * cxe_ref[j, pl.ds(hb - 1, tq), :]
                + ca_ref[1:2, lanes] * cxe_ref[j, pl.ds(hb, tq), :]
                + ca_ref[2:3, lanes] * cxe_ref[j, pl.ds(hb + 1, tq), :])
        ymix_ref[:, lanes] = (gb_ref[0, :, lanes].astype(F32) * conv).astype(BF16)

    kke = jnp.concatenate([kkp_ref[0], kk_ref[0], kkn_ref[0]], axis=0)
    vve = jnp.concatenate([vvp_ref[0], vv_ref[0], vvn_ref[0]], axis=0)
    col = lax.broadcasted_iota(jnp.int32, (1, 3 * BLOCK), 1)
    low_half = lax.broadcasted_iota(jnp.int32, (BLOCK, LANES), 1) < HEAD_DIM_B
    for j in range(nb):
        edge = None
        if j == 0:
            edge = jnp.where((col < BLOCK) & (i == 0), NEG, 0.0).astype(F32)
        if j == nb - 1:
            e2 = jnp.where((col >= 2 * BLOCK) & (i == last), NEG, 0.0).astype(F32)
            edge = e2 if edge is None else edge + e2
        rows = slice(j * BLOCK, (j + 3) * BLOCK)
        qs = slice(j * BLOCK, (j + 1) * BLOCK)
        for kv in range(N_KV_B):
            lanes0 = slice(kv * 2 * LANES, kv * 2 * LANES + LANES)
            lanes1 = slice(kv * 2 * LANES + LANES, (kv + 1) * 2 * LANES)
            kk = kke[rows, kv * LANES:(kv + 1) * LANES]
            vv = vve[rows, kv * LANES:(kv + 1) * LANES]
            q4 = jnp.concatenate([qlo_ref[0, qs, lanes0], qhi_ref[0, qs, lanes0],
                                  qlo_ref[0, qs, lanes1], qhi_ref[0, qs, lanes1]], axis=0)
            s = _dot_nt(q4, kk) + bias_ref[kv]
            if edge is not None:
                s = s + edge
            ps, rs = [], []
            for hh in range(GROUP_B):
                l = s[hh * BLOCK:(hh + 1) * BLOCK]
                sk = sink_ref[kv * GROUP_B + hh]
                m = jnp.maximum(jnp.max(l, axis=-1, keepdims=True), sk)
                p = jnp.exp(l - m)
                den = jnp.sum(p, axis=-1, keepdims=True) + jnp.exp(sk - m)
                ps.append(p.astype(BF16))
                rs.append(1.0 / den)
            o4 = _dot(jnp.concatenate(ps, axis=0), vv) * jnp.concatenate(rs, axis=0)
            c0 = D_CONV_A + kv * 2 * LANES
            ymix_ref[qs, c0:c0 + LANES] = jnp.where(
                low_half, o4[0:BLOCK], o4[BLOCK:2 * BLOCK]).astype(BF16)
            ymix_ref[qs, c0 + LANES:c0 + 2 * LANES] = jnp.where(
                low_half, o4[2 * BLOCK:3 * BLOCK], o4[3 * BLOCK:4 * BLOCK]).astype(BF16)

    t = _dot(ymix_ref[...], wout_ref[...])
    out_ref[0] = h_ref[0] + _rms(t, g_ref[...])


def _mixer_ab(h, g0, g1, w_in, conv_a, sink, w_out, bias4):
    b, s, d = h.shape
    ts = TOKEN_TILE
    grid = (b, s // ts)
    n_in = w_in.shape[1]
    bf = lambda c: jax.ShapeDtypeStruct((b, s, c), BF16)
    gb, cx, qlo, qhi, kk, vv = pl.pallas_call(
        _inproj_kernel,
        grid=grid,
        in_specs=[_tile_spec(ts, d), _const_spec((1, d)), _const_spec((d, n_in))],
        out_specs=[_tile_spec(ts, D_CONV_A), _tile_spec(ts, D_CONV_A), _tile_spec(ts, D_Q_B),
                   _tile_spec(ts, D_Q_B), _tile_spec(ts, 2 * LANES), _tile_spec(ts, 2 * LANES)],
        out_shape=[bf(D_CONV_A), bf(D_CONV_A), bf(D_Q_B), bf(D_Q_B), bf(2 * LANES), bf(2 * LANES)],
        compiler_params=_params(48),
        name="mixer_ab_inproj",
    )(h, g0, w_in)

    hb = SUBLANES_BF16
    return pl.pallas_call(
        _mixer_kernel,
        grid=grid,
        in_specs=[
            pl.BlockSpec(memory_space=pltpu.SMEM),
            _tile_spec(ts, d),
            _tile_spec(ts, D_CONV_A),
            _prev_spec(ts, hb, D_CONV_A), _tile_spec(ts, D_CONV_A), _next_spec(ts, hb, D_CONV_A, s),
            _tile_spec(ts, D_Q_B), _tile_spec(ts, D_Q_B),
            _prev_spec(ts, BLOCK, 2 * LANES), _tile_spec(ts, 2 * LANES), _next_spec(ts, BLOCK, 2 * LANES, s),
            _prev_spec(ts, BLOCK, 2 * LANES), _tile_spec(ts, 2 * LANES), _next_spec(ts, BLOCK, 2 * LANES, s),
            _const_spec(bias4.shape), _const_spec(conv_a.shape), _const_spec(w_out.shape), _const_spec((1, d)),
        ],
        out_specs=_tile_spec(ts, d),
        out_shape=jax.ShapeDtypeStruct((b, s, d), F32),
        scratch_shapes=[pltpu.VMEM((ts, D_CONV_A + D_Q_B), BF16),
                        pltpu.VMEM((D_CONV_A // LANES, ts + 2 * hb, LANES), F32)],
        compiler_params=_params(48),
        name="mixer_ab_core",
    )(sink, h, gb, cx, cx, cx, qlo, qhi, kk, kk, kk, vv, vv, vv, bias4, conv_a, w_out, g1)


def _slab_spec(rows, row_block):
    return pl.BlockSpec((1, D_MODEL // LANES, rows, LANES), lambda b, i: (b, 0, row_block(i), 0))


def _glu_kernel(x_ref, g_ref, w_ref, u_ref):
    hn = _rms(x_ref[0], g_ref[...]).astype(BF16)
    u = _dot(hn, w_ref[:, 0:D_MODEL]) * jax.nn.sigmoid(_dot(hn, w_ref[:, D_MODEL:2 * D_MODEL]))
    for j in range(D_MODEL // LANES):
        u_ref[0, j] = u[:, j * LANES:(j + 1) * LANES]


def _conf_kernel(h_ref, up_ref, u_ref, un_ref, cw_ref, lg_ref, lb_ref, w_ref, g_ref, out_ref,
                 edge_ref, cv_ref):
    i = pl.program_id(1)
    last = pl.num_programs(1) - 1
    ts = h_ref.shape[1]
    hb = CONV_C_HALO
    rc = CONV_C_ROWS
    pad = CONV_C_W // 2
    slabs = D_MODEL // LANES
    for j in range(slabs):
        edge_ref[0, j, 0:hb] = jnp.where(i > 0, up_ref[0, j], 0.0)
        edge_ref[0, j, hb:2 * hb + rc] = u_ref[0, j, 0:hb + rc]
        edge_ref[1, j, 0:hb + rc] = u_ref[0, j, ts - hb - rc:ts]
        edge_ref[1, j, hb + rc:2 * hb + rc] = jnp.where(i < last, un_ref[0, j], 0.0)

    def conv_rows(load, r0):
        for j in range(slabs):
            lanes = slice(j * LANES, (j + 1) * LANES)
            acc = [cw_ref[k:k + 1, lanes] * load(j, k) for k in range(2)]
            for k in range(2, CONV_C_W):
                acc[k % 2] = acc[k % 2] + cw_ref[k:k + 1, lanes] * load(j, k)
            cv_ref[pl.ds(r0, rc), lanes] = acc[0] + acc[1]

    conv_rows(lambda j, k: edge_ref[0, j, pl.ds(k + hb - pad, rc), :], 0)
    conv_rows(lambda j, k: edge_ref[1, j, pl.ds(k + hb - pad, rc), :], ts - rc)

    def rows_step(r, carry):
        r0 = pl.multiple_of(r * rc, rc)
        conv_rows(lambda j, k: u_ref[0, j, pl.ds(r0 + (k - pad), rc), :], r0)
        return carry

    lax.fori_loop(1, ts // rc - 1, rows_step, 0)
    x = cv_ref[...]
    xc = x - jnp.mean(x, axis=-1, keepdims=True)
    y = xc * lax.rsqrt(jnp.mean(xc * xc, axis=-1, keepdims=True) + EPS) * lg_ref[...] + lb_ref[...]
    t = _dot((y * jax.nn.sigmoid(y)).astype(BF16), w_ref[...])
    out_ref[0] = h_ref[0] + _rms(t, g_ref[...])


def _conformer(h, g0, g1, w_pw1, conv_c, ln_g, ln_b, w_pw2):
    b, s, d = h.shape
    ts = TOKEN_TILE
    grid = (b, s // ts)
    u = pl.pallas_call(
        _glu_kernel,
        grid=grid,
        in_specs=[_tile_spec(ts, d), _const_spec((1, d)), _const_spec(w_pw1.shape)],
        out_specs=_slab_spec(ts, lambda i: i),
        out_shape=jax.ShapeDtypeStruct((b, d // LANES, s, LANES), F32),
        compiler_params=_params(48),
        name="conformer_glu",
    )(h, g0, w_pw1)
    hb = CONV_C_HALO
    r = ts // hb
    nblk = s // hb
    return pl.pallas_call(
        _conf_kernel,
        grid=grid,
        in_specs=[_tile_spec(ts, d),
                  _slab_spec(hb, lambda i: jnp.maximum(i * r - 1, 0)),
                  _slab_spec(ts, lambda i: i),
                  _slab_spec(hb, lambda i: jnp.minimum((i + 1) * r, nblk - 1)),
                  _const_spec(conv_c.shape), _const_spec((1, d)), _const_spec((1, d)),
                  _const_spec(w_pw2.shape), _const_spec((1, d))],
        out_specs=_tile_spec(ts, d),
        out_shape=jax.ShapeDtypeStruct((b, s, d), F32),
        scratch_shapes=[pltpu.VMEM((2, d // LANES, CONV_C_ROWS + 2 * hb, LANES), F32), pltpu.VMEM((ts, d), F32)],
        compiler_params=_params(48),
        name="conformer_conv",
    )(h, u, u, u, conv_c, ln_g, ln_b, w_pw2, g1)


def _memkv_kernel(mem_ref, w_ref, k_ref, v_ref):
    kv = _dot(mem_ref[0].astype(BF16), w_ref[0])
    k_ref[0, 0] = kv[:, 0:D_X].astype(BF16)
    v_ref[0, 0] = kv[:, D_X:2 * D_X].astype(BF16)


def _mem_kv(mem, w_xkv):
    b, nm, d = mem.shape
    shape = jax.ShapeDtypeStruct((DEPTH, b, nm, D_X), BF16)
    spec = pl.BlockSpec((1, 1, nm, D_X), lambda l, bb: (l, bb, 0, 0))
    return pl.pallas_call(
        _memkv_kernel,
        grid=(DEPTH, b),
        in_specs=[pl.BlockSpec((1, nm, d), lambda l, bb: (bb, 0, 0)),
                  pl.BlockSpec((1, d, 2 * D_X), lambda l, bb: (l, 0, 0))],
        out_specs=[spec, spec],
        out_shape=[shape, shape],
        compiler_params=_params(32),
        name="mem_kv",
    )(mem, w_xkv)


def _xattn_kernel(h_ref, k_ref, v_ref, wq_ref, wo_ref, g2_ref, g3_ref, out_ref):
    x = h_ref[0]
    hn = _rms(x, g2_ref[...]).astype(BF16)
    q = (_dot(hn, wq_ref[...]) * np.float32(HEAD_DIM_X ** -0.5)).astype(BF16)
    outs = []
    for hh in range(N_HEADS_X):
        lanes = slice(hh * HEAD_DIM_X, (hh + 1) * HEAD_DIM_X)
        s = _dot_nt(q[:, lanes], k_ref[0, 0, :, lanes])
        p = jnp.exp(s - jnp.max(s, axis=-1, keepdims=True))
        r = 1.0 / jnp.sum(p, axis=-1, keepdims=True)
        outs.append((_dot(p.astype(BF16), v_ref[0, 0, :, lanes]) * r).astype(BF16))
    t = _dot(jnp.concatenate(outs, axis=1), wo_ref[...])
    out_ref[0] = x + _rms(t, g3_ref[...])


def _xattn(h, layer, kmem, vmem, g2, g3, w_xq, w_xo):
    b, s, d = h.shape
    ts = TOKEN_TILE
    nm = kmem.shape[2]
    kv_spec = pl.BlockSpec((1, 1, nm, D_X), lambda bb, i: (layer, bb, 0, 0))
    return pl.pallas_call(
        _xattn_kernel,
        grid=(b, s // ts),
        in_specs=[_tile_spec(ts, d), kv_spec, kv_spec, _layer_spec(w_xq.shape, layer),
                  _layer_spec(w_xo.shape, layer), _const_spec((1, d)), _const_spec((1, d))],
        out_specs=_tile_spec(ts, d),
        out_shape=jax.ShapeDtypeStruct((b, s, d), F32),
        compiler_params=_params(48),
        name="mem_xattn",
    )(h, kmem, vmem, w_xq, w_xo, g2, g3)


def _ffn_kernel(hp_ref, h_ref, hn_ref, g4_ref, wup_ref, cf_ref, wdn_ref, g5_ref, out_ref,
                xn_ref, u_ref, act_ref):
    i = pl.program_id(1)
    last = pl.num_programs(1) - 1
    ts = h_ref.shape[1]
    hb = hp_ref.shape[1]
    half = FF_CHUNK // LANES
    x_ext = jnp.concatenate([jnp.where(i > 0, hp_ref[0], 0.0), h_ref[0], jnp.where(i < last, hn_ref[0], 0.0)],
                            axis=0)
    xn_ref[...] = _rms(x_ext, g4_ref[...]).astype(BF16)
    for c in range(N_FF_CHUNKS):
        slot = c % 2
        cols = (c * FF_CHUNK, D_FF + c * FF_CHUNK)
        centre = {}
        for part, c0 in enumerate(cols):
            u = _dot(xn_ref[...], wup_ref[:, c0:c0 + FF_CHUNK])
            for j in range(half):
                uj = u[:, j * LANES:(j + 1) * LANES]
                u_ref[slot, part * half + j] = uj
                centre[part, j] = cf_ref[1:2, c0 + j * LANES:c0 + (j + 1) * LANES] * uj[hb:hb + ts]

        def conv3(part, j):
            lanes = slice(cols[part] + j * LANES, cols[part] + (j + 1) * LANES)
            slab = part * half + j
            return (centre[part, j]
                    + cf_ref[0:1, lanes] * u_ref[slot, slab, pl.ds(hb - 1, ts), :]
                    + cf_ref[2:3, lanes] * u_ref[slot, slab, pl.ds(hb + 1, ts), :])

        for j in range(half):
            gate = conv3(0, j)
            up = conv3(1, j)
            c0 = c * FF_CHUNK + j * LANES
            act_ref[:, c0:c0 + LANES] = (gate * jax.nn.sigmoid(gate) * up).astype(BF16)
    out_ref[0] = h_ref[0] + _rms(_dot(act_ref[...], wdn_ref[...]), g5_ref[...])


def _conv_ffn(h, layer, g4, g5, w_up, conv_f, w_down):
    b, s, d = h.shape
    ts = FFN_TOKEN_TILE
    hb = SUBLANES_F32
    return pl.pallas_call(
        _ffn_kernel,
        grid=(b, s // ts),
        in_specs=[_prev_spec(ts, hb, d), _tile_spec(ts, d), _next_spec(ts, hb, d, s),
                  _const_spec((1, d)), _layer_spec(w_up.shape, layer), _layer_spec(conv_f.shape, layer),
                  _layer_spec(w_down.shape, layer), _const_spec((1, d))],
        out_specs=_tile_spec(ts, d),
        out_shape=jax.ShapeDtypeStruct((b, s, d), F32),
        scratch_shapes=[pltpu.VMEM((ts + 2 * hb, d), BF16),
                        pltpu.VMEM((2, 2 * FF_CHUNK // LANES, ts + 2 * hb, LANES), F32),
                        pltpu.VMEM((ts, D_FF), BF16)],
        compiler_params=_params(56),
        name="conv_ffn",
    )(h, h, h, g4, w_up, conv_f, w_down, g5)


def _t5_bucket(rel):
    half = N_BUCKETS // 2
    max_exact = half // 2
    ret = (rel > 0).astype(jnp.int32) * half
    n = jnp.abs(rel)
    nf = jnp.maximum(n, 1).astype(jnp.float32)
    large = max_exact + (jnp.log(nf / max_exact) / np.float32(np.log(MAX_DIST / max_exact))
                         * (half - max_exact)).astype(jnp.int32)
    large = jnp.minimum(large, half - 1)
    return ret + jnp.where(n < max_exact, n, large)


def _band_bias(rel_table):
    qi = jnp.arange(BLOCK)[:, None]
    kj = jnp.arange(3 * BLOCK)[None, :]
    rel = kj - BLOCK - qi
    hit = _t5_bucket(rel)[None, :, :, None] == jnp.arange(N_BUCKETS)
    bias = jnp.sum(jnp.where(hit, rel_table.astype(F32).T[:, None, None, :], 0.0), axis=-1)
    bias = jnp.where((jnp.abs(rel) <= WINDOW)[None], bias, NEG)
    return bias.reshape(N_KV_B, GROUP_B * BLOCK, 3 * BLOCK)


def _dup_heads(w):
    d = w.shape[0]
    w = w.reshape(d, N_KV_B, 1, HEAD_DIM_B)
    return jnp.broadcast_to(w, (d, N_KV_B, 2, HEAD_DIM_B)).reshape(d, N_KV_B * 2 * HEAD_DIM_B)


def _prep_w_in(w):
    o = 3 * D_CONV_A + D_Q_B
    dkv = N_KV_B * HEAD_DIM_B
    return jnp.concatenate([w[:, :o], _dup_heads(w[:, o:o + dkv]), _dup_heads(w[:, o + dkv:])],
                           axis=1).astype(BF16)


def _trunk(h, mem, p):
    kmem, vmem = _mem_kv(mem, p["w_xkv"])
    for i in range(DEPTH):
        g = p["norm_g"][i]
        j = i // 2
        if i % 2 == 0:
            h = _mixer_ab(h, g[0], g[1], p["w_in"][j], p["conv_a"][j], p["sink"][j], p["w_out"][j], p["bias4"])
        else:
            h = _conformer(h, g[0], g[1], p["w_pw1"][j], p["conv_c"][j], p["ln_g"][j], p["ln_b"][j], p["w_pw2"][j])
        h = _xattn(h, i, kmem, vmem, g[2], g[3], p["w_xq"], p["w_xo"])
        h = _conv_ffn(h, i, g[4], g[5], p["w_up"], p["conv_f"], p["w_down"])
    return h


def kernel(x_prompt, x_sample, mem_prompt, mem_sample, norm_g, rel_bias, w_in_ab, conv_a, sink_b, w_out_ab,
           w_pw1_c, conv_c, ln_g_c, ln_b_c, w_pw2_c, w_xq, w_xkv, w_xo, w_up, conv_f, w_down):
    n_even = w_in_ab.shape[0]
    n_odd = w_pw1_c.shape[0]
    p = {
        "norm_g": norm_g.astype(F32)[:, :, None, :],
        "bias4": _band_bias(rel_bias),
        "w_in": [_prep_w_in(w_in_ab[j]) for j in range(n_even)],
        "conv_a": conv_a.astype(F32),
        "sink": sink_b.astype(F32),
        "w_out": w_out_ab.astype(BF16),
        "w_pw1": w_pw1_c.astype(BF16),
        "conv_c": conv_c.astype(F32),
        "ln_g": ln_g_c.astype(F32)[:, None, :],
        "ln_b": ln_b_c.astype(F32)[:, None, :],
        "w_pw2": w_pw2_c.astype(BF16),
        "w_xq": w_xq.astype(BF16),
        "w_xkv": w_xkv.astype(BF16),
        "w_xo": w_xo.astype(BF16),
        "w_up": w_up.astype(BF16),
        "conv_f": conv_f.astype(F32),
        "w_down": w_down.astype(BF16),
    }
    return (_trunk(x_prompt, mem_prompt, p), _trunk(x_sample, mem_sample, p))
```
